```python
import math
import jax, jax.numpy as jnp
from jax import lax
import numpy as np

D_MODEL = 2048
BATCH = 1
SEQ = 8192
DEPTH = 2
DEC_BATCH = 32
DEC_SEQ = 8
PAST_LEN = 8192
PAGE_SIZE = 128

N_MIXERS = 2
N_LRU_LAYERS = (DEPTH + 1) // 2
N_NSA_LAYERS = DEPTH // 2
D_RNN = D_MODEL
LRU_BLOCK = 256
N_LRU_BLOCKS = D_RNN // LRU_BLOCK
CONV_W = 4
LRU_C = 8.0
HEAD_DIM = 128
N_HEADS = D_MODEL // HEAD_DIM
N_KV_HEADS = 4
HEADS_PER_KV = N_HEADS // N_KV_HEADS
CMP_BLOCK = 32
CMP_STRIDE = 16
CMP_HIDDEN = HEAD_DIM
SEL_BLOCK = 64
N_SEL = 16
WINDOW = 512
N_BRANCH = 3
Q_BLOCK = 128
ROPE_THETA = 10000.0
NSA_IN = N_HEADS * HEAD_DIM + 2 * N_BRANCH * N_KV_HEADS * HEAD_DIM + N_BRANCH * N_HEADS
N_GROUPS = 4
EXPERTS_PER_GROUP = 8
N_EXPERTS = N_GROUPS * EXPERTS_PER_GROUP
TOP_K_IN_GROUP = 2
EXPERT_FF = 1024
PLE_DIM = 256
DEEPNORM_ALPHA = (2 * DEPTH) ** 0.25
DEEPNORM_BETA = (8 * DEPTH) ** -0.25
LN_EPS = 1e-5
NEG_INF = -1e30
FORCE_BONUS = 1e4

kernel_name = 'hybrid_rglru_nsa_hmoe_step'


def rope(x, pos):
    half = HEAD_DIM // 2
    inv = ROPE_THETA ** (-jnp.arange(half, dtype=jnp.float32) / half)
    ang = pos.astype(jnp.float32)[:, None] * inv[None, :]
    cos = jnp.cos(ang)[None, :, None, :]
    sin = jnp.sin(ang)[None, :, None, :]
    xf = x.astype(jnp.float32)
    x1, x2 = xf[..., :half], xf[..., half:]
    return jnp.concatenate([x1 * cos - x2 * sin, x2 * cos + x1 * sin], axis=-1).astype(x.dtype)


def layer_norm(x, g, b):
    xf = x.astype(jnp.float32)
    mu = jnp.mean(xf, axis=-1, keepdims=True)
    var = jnp.mean(jnp.square(xf - mu), axis=-1, keepdims=True)
    return ((xf - mu) * lax.rsqrt(var + LN_EPS) * g + b).astype(x.dtype)


def rglru_mixer(x, conv_buf, h0, w_in, conv_w, conv_b, w_a, b_a, w_x, b_x, lam, w_out):
    B, T, _ = x.shape
    u = x @ w_in
    gate = jax.nn.gelu(u[..., :D_RNN])
    xr = u[..., D_RNN:]
    xp = jnp.concatenate([conv_buf.astype(xr.dtype), xr], axis=1)
    xc = conv_b + xp[:, 0:T] * conv_w[0]
    for k in range(1, CONV_W):
        xc = xc + xp[:, k:k + T] * conv_w[k]
    xb = xc.reshape(B, T, N_LRU_BLOCKS, LRU_BLOCK)
    r = jax.nn.sigmoid(jnp.einsum('btnc,ncd->btnd', xb, w_a).reshape(B, T, D_RNN) + b_a)
    i = jax.nn.sigmoid(jnp.einsum('btnc,ncd->btnd', xb, w_x).reshape(B, T, D_RNN) + b_x)
    log_a = (-LRU_C * r.astype(jnp.float32)) * jax.nn.softplus(-lam.astype(jnp.float32))
    a = jnp.exp(log_a)
    bterm = jnp.sqrt(-jnp.expm1(2.0 * log_a)) * (i * xc).astype(jnp.float32)

    def combine(lhs, rhs):
        a1, b1 = lhs
        a2, b2 = rhs
        return a1 * a2, a2 * b1 + b2

    a_cum, b_cum = lax.associative_scan(combine, (a, bterm), axis=1)
    h = a_cum * h0.astype(jnp.float32)[:, None, :] + b_cum
    y = (h.astype(x.dtype) * gate) @ w_out
    return y, xp[:, -(CONV_W - 1):], h[:, -1]


def nsa_compress(k, w1, pe, w2):
    B, T = k.shape[:2]
    nc = (T - CMP_BLOCK) // CMP_STRIDE + 1
    n_chunk = -(-T // CMP_STRIDE)
    parts = CMP_BLOCK // CMP_STRIDE
    kp = jnp.pad(k, ((0, 0), (0, n_chunk * CMP_STRIDE - T), (0, 0), (0, 0)))
    chunks = kp.reshape(B, n_chunk, CMP_STRIDE, N_KV_HEADS, HEAD_DIM)
    w1r = w1.reshape(parts, CMP_STRIDE, HEAD_DIM, CMP_HIDDEN)
    proj = jnp.einsum('bnsgd,msdh->mbngh', chunks, w1r)
    pre = proj[0, :, 0:nc]
    for m in range(1, parts):
        pre = pre + proj[m, :, m:m + nc]
    pre = pre + jnp.einsum('ld,ldh->h', pe, w1)
    return jax.nn.gelu(pre) @ w2


def nsa_attend(q, gates, kc, vc, ks, vs, kw, vw, q_pos0, kw_pos0):
    B, Tq = q.shape[:2]
    T = ks.shape[1]
    nc = kc.shape[1]
    ns = -(-T // SEL_BLOCK)
    n_top = min(N_SEL, ns)
    qb = math.gcd(Tq, Q_BLOCK)
    n_qb = Tq // qb
    n_tok = n_top * SEL_BLOCK
    scale = HEAD_DIM ** -0.5
    cmp_end = jnp.arange(nc) * CMP_STRIDE + (CMP_BLOCK - 1)
    ratio = SEL_BLOCK // CMP_STRIDE
    offs = jnp.arange(1 - CMP_BLOCK // CMP_STRIDE, ratio)
    ov_idx = jnp.arange(ns)[:, None] * ratio + offs[None, :]
    ov_ok = (ov_idx >= 0) & (ov_idx < nc)
    ov_idx = jnp.clip(ov_idx, 0, nc - 1)
    sel_start = jnp.arange(ns) * SEL_BLOCK
    blk = jnp.arange(ns)
    pad_t = ns * SEL_BLOCK - T
    ks_t = jnp.pad(ks, ((0, 0), (0, pad_t), (0, 0), (0, 0))).transpose(0, 2, 1, 3)
    vs_t = jnp.pad(vs, ((0, 0), (0, pad_t), (0, 0), (0, 0))).transpose(0, 2, 1, 3)
    kw_p = jnp.pad(kw, ((0, 0), (WINDOW, 0), (0, 0), (0, 0)))
    vw_p = jnp.pad(vw, ((0, 0), (WINDOW, 0), (0, 0), (0, 0)))
    gather_rows = jax.vmap(jax.vmap(lambda arr, idx: arr[idx]))
    q_blocks = q.reshape(B, n_qb, qb, N_KV_HEADS, HEADS_PER_KV, HEAD_DIM).transpose(1, 0, 2, 3, 4, 5)
    g_blocks = gates.reshape(B, n_qb, qb, N_KV_HEADS, HEADS_PER_KV, N_BRANCH).transpose(1, 0, 2, 3, 4, 5)

    def masked_softmax(s, mask):
        s = jnp.where(mask, s.astype(jnp.float32) * scale, NEG_INF)
        return jnp.where(mask, jax.nn.softmax(s, axis=-1), 0.0)

    def one_block(args):
        qg, gg, bi = args
        q0 = q_pos0 + bi * qb
        t = q0 + jnp.arange(qb)
        m_c = (cmp_end[None, :] <= t[:, None])[None, :, None, None, :]
        p_c = masked_softmax(jnp.einsum('bqghd,bngd->bqghn', qg, kc), m_c)
        o_c = jnp.einsum('bqghn,bngd->bqghd', p_c.astype(vc.dtype), vc)
        imp = jnp.sum(p_c, axis=3)
        imp_sel = jnp.sum(jnp.where(ov_ok, imp[..., ov_idx], 0.0), axis=-1)
        causal = (sel_start[None, :] <= t[:, None])[None, :, None, :]
        forced = ((blk[None, :] == (t // SEL_BLOCK)[:, None]) | (blk[None, :] == 0))[None, :, None, :]
        score = jnp.where(causal, imp_sel + jnp.where(forced, FORCE_BONUS, 0.0), NEG_INF)
        _, top = lax.top_k(score, n_top)
        pos = (top[..., None] * SEL_BLOCK + jnp.arange(SEL_BLOCK)).reshape(B, qb, N_KV_HEADS, n_tok)
        pos_t = pos.transpose(0, 2, 1, 3).reshape(B, N_KV_HEADS, qb * n_tok)
        k_g = gather_rows(ks_t, pos_t).reshape(B, N_KV_HEADS, qb, n_tok, HEAD_DIM)
        v_g = gather_rows(vs_t, pos_t).reshape(B, N_KV_HEADS, qb, n_tok, HEAD_DIM)
        m_s = (pos <= t[None, :, None, None])[:, :, :, None, :]
        p_s = masked_softmax(jnp.einsum('bqghd,bgqkd->bqghk', qg, k_g), m_s)
        o_s = jnp.einsum('bqghk,bgqkd->bqghd', p_s.astype(v_g.dtype), v_g)
        start = q0 - kw_pos0
        k_w = lax.dynamic_slice_in_dim(kw_p, start, WINDOW + qb, axis=1)
        v_w = lax.dynamic_slice_in_dim(vw_p, start, WINDOW + qb, axis=1)
        kpos = q0 - WINDOW + jnp.arange(WINDOW + qb)
        m_w = (kpos[None, :] <= t[:, None]) & (kpos[None, :] > t[:, None] - WINDOW) & (kpos[None, :] >= kw_pos0)
        p_w = masked_softmax(jnp.einsum('bqghd,bkgd->bqghk', qg, k_w), m_w[None, :, None, None, :])
        o_w = jnp.einsum('bqghk,bkgd->bqghd', p_w.astype(v_w.dtype), v_w)
        o = gg[..., 0:1] * o_c + gg[..., 1:2] * o_s + gg[..., 2:3] * o_w
        return o.reshape(B, qb, N_HEADS * HEAD_DIM)

    out = lax.map(one_block, (q_blocks, g_blocks, jnp.arange(n_qb)))
    return out.transpose(1, 0, 2, 3).reshape(B, Tq, N_HEADS * HEAD_DIM)


def nsa_mixer(x, past_rows, past_win, pos0, w_in, cmp_w1, cmp_pe, cmp_w2, w_out):
    B, T, _ = x.shape
    hd = N_HEADS * HEAD_DIM
    kvd = 2 * N_BRANCH * N_KV_HEADS * HEAD_DIM
    u = x @ w_in
    pos = pos0 + jnp.arange(T)
    q = rope(u[..., :hd].reshape(B, T, N_HEADS, HEAD_DIM), pos)
    kv = u[..., hd:hd + kvd].reshape(B, T, N_BRANCH, 2, N_KV_HEADS, HEAD_DIM)
    k_rot = rope(kv[:, :, :, 0].reshape(B, T, N_BRANCH * N_KV_HEADS, HEAD_DIM), pos)
    kv = jnp.stack([k_rot.reshape(B, T, N_BRANCH, N_KV_HEADS, HEAD_DIM), kv[:, :, :, 1]], axis=3)
    gates = jax.nn.sigmoid(u[..., hd + kvd:].reshape(B, T, N_HEADS, N_BRANCH))
    new_rows = kv[:, :, :2].reshape(B, T, 4, N_KV_HEADS, HEAD_DIM)
    new_win = kv[:, :, 2]
    if past_rows is None:
        rows, win, kw_pos0, win_keep = new_rows, new_win, 0, min(WINDOW, T)
    else:
        rows = jnp.concatenate([past_rows.astype(new_rows.dtype), new_rows], axis=1)
        win = jnp.concatenate([past_win.astype(new_win.dtype), new_win], axis=1)
        win_keep = past_win.shape[1]
        kw_pos0 = pos0 - win_keep
    kc = nsa_compress(rows[:, :, 0], cmp_w1[0], cmp_pe[0], cmp_w2[0])
    vc = nsa_compress(rows[:, :, 1], cmp_w1[1], cmp_pe[1], cmp_w2[1])
    o = nsa_attend(q, gates, kc, vc, rows[:, :, 2], rows[:, :, 3], win[:, :, 0], win[:, :, 1], pos0, kw_pos0)
    return o @ w_out, new_rows, win[:, -win_keep:]


def hier_moe(x, w_group, b_group, w_expert, b_expert, w_gate, w_up, w_down):
    B, T, D = x.shape
    xf = x.reshape(B * T, D)
    p_grp = jax.nn.softmax((xf @ w_group + b_group).astype(jnp.float32), axis=-1)
    p_top, g_top = lax.top_k(p_grp, 1)
    g_sel = g_top[:, 0]
    logit_e = (xf @ w_expert + b_expert).astype(jnp.float32).reshape(-1, N_GROUPS, EXPERTS_PER_GROUP)
    logit_in = logit_e[jnp.arange(xf.shape[0]), g_sel]
    v2, j2 = lax.top_k(logit_in, TOP_K_IN_GROUP)
    w2 = jax.nn.softmax(v2, axis=-1) * p_top
    comb = jnp.sum(jax.nn.one_hot(j2, EXPERTS_PER_GROUP, dtype=jnp.float32) * w2[..., None], axis=1)
    y = jnp.zeros_like(xf)
    for grp in range(N_GROUPS):
        sl = slice(grp * EXPERTS_PER_GROUP, (grp + 1) * EXPERTS_PER_GROUP)
        wg = comb * (g_sel == grp)[:, None]
        h = jax.nn.silu(jnp.einsum('nd,edf->nef', xf, w_gate[sl])) * jnp.einsum('nd,edf->nef', xf, w_up[sl])
        y = y + jnp.einsum('nef,efd->nd', h * wg[..., None].astype(h.dtype), w_down[sl])
    return y.reshape(B, T, D)


def run_trunk(x, p, rec_state, attn_state, pos0, lru_w, nsa_w, layer_w):
    (ln1_g, ln1_b, ln2_g, ln2_b, moe_w_group, moe_b_group, moe_w_expert, moe_b_expert,
     moe_w_gate, moe_w_up, moe_w_down, ple_w_gate, ple_w_proj) = layer_w
    B = x.shape[0]
    conv_out, h_out, rows_out, win_out = [], [], [], []
    for i in range(DEPTH):
        l = i // N_MIXERS
        if i % N_MIXERS == 0:
            if rec_state is None:
                conv0 = jnp.zeros((B, CONV_W - 1, D_RNN), x.dtype)
                h0 = jnp.zeros((B, D_RNN), jnp.float32)
            else:
                conv0, h0 = rec_state[0][l], rec_state[1][l]
            mix, conv_new, h_new = rglru_mixer(x, conv0, h0, *[w[l] for w in lru_w])
            conv_out.append(conv_new)
            h_out.append(h_new)
        else:
            if attn_state is None:
                past_rows, past_win = None, None
            else:
                cache_kv, cache_win, page_table = attn_state
                n_pages = PAST_LEN // PAGE_SIZE
                past_rows = cache_kv[l][page_table].reshape(B, n_pages * PAGE_SIZE, 4, N_KV_HEADS, HEAD_DIM)
                past_win = cache_win[l]
            mix, rows_new, win_new = nsa_mixer(x, past_rows, past_win, pos0, *[w[l] for w in nsa_w])
            rows_out.append(rows_new)
            win_out.append(win_new)
        x = layer_norm(DEEPNORM_ALPHA * x + mix, ln1_g[i], ln1_b[i])
        ffn = hier_moe(x, moe_w_group[i], moe_b_group[i], moe_w_expert[i], moe_b_expert[i],
                       moe_w_gate[i], moe_w_up[i], moe_w_down[i])
        x = layer_norm(DEEPNORM_ALPHA * x + ffn, ln2_g[i], ln2_b[i])
        x = x + jax.nn.sigmoid(x @ ple_w_gate[i]) * (p[i].astype(x.dtype) @ ple_w_proj[i])
    return x, (jnp.stack(conv_out), jnp.stack(h_out), jnp.stack(rows_out), jnp.stack(win_out))


def setup_inputs(seed: int = 0) -> dict:
    key = jax.random.key(seed)
    keys = iter(jax.random.split(key, 64))

    def nrm(shape, scale=1.0):
        return jax.random.normal(next(keys), shape, jnp.float32) * scale

    n_pages = PAST_LEN // PAGE_SIZE
    n_pool = (DEC_BATCH * n_pages * 5) // 4
    win_buf = min(WINDOW, PAST_LEN)
    NL, NN = N_LRU_LAYERS, N_NSA_LAYERS
    u = jax.random.uniform(next(keys), (NL, D_RNN), dtype=jnp.float32, minval=0.9, maxval=0.999)
    page_table = jax.random.permutation(next(keys), n_pool)[:DEC_BATCH * n_pages].reshape(DEC_BATCH, n_pages).astype(jnp.int32)
    return {
        'x_prompt': nrm((BATCH, SEQ, D_MODEL)),
        'x_sample': nrm((DEC_BATCH, DEC_SEQ, D_MODEL)),
        'state_conv': nrm((NL, DEC_BATCH, CONV_W - 1, D_RNN)),
        'state_h': nrm((NL, DEC_BATCH, D_RNN), 0.5),
        'cache_kv': nrm((NN, n_pool, PAGE_SIZE, 4, N_KV_HEADS, HEAD_DIM)),
        'cache_win': nrm((NN, DEC_BATCH, win_buf, 2, N_KV_HEADS, HEAD_DIM)),
        'page_table': page_table,
        'p_prompt': nrm((DEPTH, BATCH, SEQ, PLE_DIM)),
        'p_sample': nrm((DEPTH, DEC_BATCH, DEC_SEQ, PLE_DIM)),
        'lru_w_in': nrm((NL, D_MODEL, 2 * D_RNN), D_MODEL ** -0.5),
        'lru_conv_w': nrm((NL, CONV_W, D_RNN), CONV_W ** -0.5),
        'lru_conv_b': nrm((NL, D_RNN), 0.02),
        'lru_w_a': nrm((NL, N_LRU_BLOCKS, LRU_BLOCK, LRU_BLOCK), LRU_BLOCK ** -0.5),
        'lru_b_a': nrm((NL, D_RNN), 0.1),
        'lru_w_x': nrm((NL, N_LRU_BLOCKS, LRU_BLOCK, LRU_BLOCK), LRU_BLOCK ** -0.5),
        'lru_b_x': nrm((NL, D_RNN), 0.1),
        'lru_lambda': jnp.log(u) - jnp.log1p(-u),
        'lru_w_out': nrm((NL, D_RNN, D_MODEL), D_RNN ** -0.5 * DEEPNORM_BETA),
        'nsa_w_in': nrm((NN, D_MODEL, NSA_IN), D_MODEL ** -0.5),
        'nsa_cmp_w1': nrm((NN, 2, CMP_BLOCK, HEAD_DIM, CMP_HIDDEN), (CMP_BLOCK * HEAD_DIM) ** -0.5),
        'nsa_cmp_pe': nrm((NN, 2, CMP_BLOCK, HEAD_DIM), 0.5),
        'nsa_cmp_w2': nrm((NN, 2, CMP_HIDDEN, HEAD_DIM), CMP_HIDDEN ** -0.5),
        'nsa_w_out': nrm((NN, N_HEADS * HEAD_DIM, D_MODEL), (N_HEADS * HEAD_DIM) ** -0.5 * DEEPNORM_BETA),
        'ln1_g': 1.0 + nrm((DEPTH, D_MODEL), 0.02),
        'ln1_b': nrm((DEPTH, D_MODEL), 0.02),
        'ln2_g': 1.0 + nrm((DEPTH, D_MODEL), 0.02),
        'ln2_b': nrm((DEPTH, D_MODEL), 0.02),
        'moe_w_group': nrm((DEPTH, D_MODEL, N_GROUPS), D_MODEL ** -0.5),
        'moe_b_group': nrm((DEPTH, N_GROUPS), 0.01),
        'moe_w_expert': nrm((DEPTH, D_MODEL, N_EXPERTS), D_MODEL ** -0.5),
        'moe_b_expert': nrm((DEPTH, N_EXPERTS), 0.01),
        'moe_w_gate': nrm((DEPTH, N_EXPERTS, D_MODEL, EXPERT_FF), D_MODEL ** -0.5),
        'moe_w_up': nrm((DEPTH, N_EXPERTS, D_MODEL, EXPERT_FF), D_MODEL ** -0.5),
        'moe_w_down': nrm((DEPTH, N_EXPERTS, EXPERT_FF, D_MODEL), EXPERT_FF ** -0.5 * DEEPNORM_BETA),
        'ple_w_gate': nrm((DEPTH, D_MODEL, D_MODEL), D_MODEL ** -0.5),
        'ple_w_proj': nrm((DEPTH, PLE_DIM, D_MODEL), PLE_DIM ** -0.5),
    }


def reference(x_prompt, x_sample, state_conv, state_h, cache_kv, cache_win, page_table,
              p_prompt, p_sample,
              lru_w_in, lru_conv_w, lru_conv_b, lru_w_a, lru_b_a, lru_w_x, lru_b_x, lru_lambda, lru_w_out,
              nsa_w_in, nsa_cmp_w1, nsa_cmp_pe, nsa_cmp_w2, nsa_w_out,
              ln1_g, ln1_b, ln2_g, ln2_b,
              moe_w_group, moe_b_group, moe_w_expert, moe_b_expert, moe_w_gate, moe_w_up, moe_w_down,
              ple_w_gate, ple_w_proj):
    lru_w = (lru_w_in, lru_conv_w, lru_conv_b, lru_w_a, lru_b_a, lru_w_x, lru_b_x, lru_lambda, lru_w_out)
    nsa_w = (nsa_w_in, nsa_cmp_w1, nsa_cmp_pe, nsa_cmp_w2, nsa_w_out)
    layer_w = (ln1_g, ln1_b, ln2_g, ln2_b, moe_w_group, moe_b_group, moe_w_expert, moe_b_expert,
               moe_w_gate, moe_w_up, moe_w_down, ple_w_gate, ple_w_proj)
    y_prompt, st_p = run_trunk(x_prompt, p_prompt, None, None, 0, lru_w, nsa_w, layer_w)
    conv_prompt, h_prompt, kv_prompt, win_prompt = st_p
    y_sample, st_s = run_trunk(x_sample, p_sample, (state_conv, state_h), (cache_kv, cache_win, page_table),
                               PAST_LEN, lru_w, nsa_w, layer_w)
    conv_sample, h_sample, kv_sample, win_sample = st_s
    return (y_prompt, y_sample, conv_prompt, h_prompt, kv_prompt, win_prompt,
            conv_sample, h_sample, kv_sample, win_sample)
```

```python
import functools

import jax
import jax.numpy as jnp
from jax import lax
from jax.experimental import pallas as pl
from jax.experimental.pallas import tpu as pltpu

F32 = jnp.float32
BF = jnp.bfloat16

HEAD_DIM = 128
N_HEADS = 16
N_KV = 4
HPG = N_HEADS // N_KV
LRU_BLOCK = 256
CONV_W = 4
LRU_C = 8.0
CMP_BLOCK = 32
CMP_STRIDE = 16
SEL_BLOCK = 64
N_SEL = 16
WINDOW = 512
ROPE_THETA = 10000.0
N_GROUPS = 4
EPG = 8
N_EXPERTS = N_GROUPS * EPG
DEPTH = 2
ALPHA = (2 * DEPTH) ** 0.25
LN_EPS = 1e-5
NEG = -1e30
FORCE_BONUS = 1e4
SCALE = HEAD_DIM ** -0.5

LANES = 128
SUBLANES = 8
VMEM_LIMIT = 56 * 1024 * 1024
MOE_TM = 1024
MOE_SUB = 256
MOE_TF = 256


def _cparams(sem):
    return pltpu.CompilerParams(dimension_semantics=sem, vmem_limit_bytes=VMEM_LIMIT)


def _row_tile(n, cap):
    best = None
    for t in range(SUBLANES, cap + 1, SUBLANES):
        if n % t == 0:
            best = t
    assert best is not None, (n, cap)
    return best


def _mm_act_kernel(x_ref, w_ref, o_ref, *, act):
    u = jnp.dot(x_ref[...].astype(BF), w_ref[...], preferred_element_type=F32)
    o_ref[...] = act(u).astype(o_ref.dtype)


def _mm_act(x, w, act, out_dtype=F32, tn=512):
    m, k = x.shape
    n = w.shape[1]
    tm = _row_tile(m, 768)
    tn = min(tn, n)
    assert n % tn == 0
    return pl.pallas_call(
        functools.partial(_mm_act_kernel, act=act),
        grid=(m // tm, n // tn),
        in_specs=[pl.BlockSpec((tm, k), lambda i, j: (i, 0)),
                  pl.BlockSpec((k, tn), lambda i, j: (0, j))],
        out_specs=pl.BlockSpec((tm, tn), lambda i, j: (i, j)),
        out_shape=jax.ShapeDtypeStruct((m, n), out_dtype),
        compiler_params=_cparams(("parallel", "arbitrary")),
    )(x, w)


def _rglru_kernel(xr_ref, prev_ref, h0_ref, gate_ref, cw_ref, cb_ref, wa_ref, ba_ref, wx_ref, bx_ref,
                  lam_ref, hg_ref, hlast_ref, carry_prev, carry_h, h_scr, *, chain):
    tt, c = xr_ref.shape
    g = tt // SUBLANES
    x = xr_ref[...].reshape(g, SUBLANES, c)
    if chain:
        first = pl.program_id(1) == 0
        p0 = jnp.where(first, prev_ref[...], carry_prev[...])
        prev = jnp.concatenate([p0, x[:-1]], axis=0) if g > 1 else p0
        h_start = jnp.where(first, h0_ref[...], carry_h[...])
    else:
        prev = prev_ref[...]
    sub = lax.broadcasted_iota(jnp.int32, (g, SUBLANES, c), 1)
    cw = [cw_ref[k:k + 1, :].reshape(1, 1, c) for k in range(CONV_W)]
    xc = cb_ref[...].reshape(1, 1, c) + x * cw[CONV_W - 1]
    for s in range(1, CONV_W):
        sh = jnp.where(sub < s, pltpu.roll(prev, s, 1), pltpu.roll(x, s, 1))
        xc = xc + sh * cw[CONV_W - 1 - s]
    xc2 = xc.reshape(tt, c)
    xb = xc2.astype(BF)
    r = jax.nn.sigmoid(jnp.dot(xb, wa_ref[...], preferred_element_type=F32) + ba_ref[...])
    i = jax.nn.sigmoid(jnp.dot(xb, wx_ref[...], preferred_element_type=F32) + bx_ref[...])
    log_a = (-LRU_C * r) * jax.nn.softplus(-lam_ref[...])
    a = jnp.exp(log_a)
    bterm = jnp.sqrt(-jnp.tanh(log_a) * (a * a + 1.0)) * (i * xc2)
    a3 = a.reshape(g, SUBLANES, c)
    b3 = bterm.reshape(g, SUBLANES, c)
    for d in (1, 2, 4):
        a_sh = jnp.where(sub < d, 1.0, pltpu.roll(a3, d, 1))
        b_sh = jnp.where(sub < d, 0.0, pltpu.roll(b3, d, 1))
        b3 = a3 * b_sh + b3
        a3 = a3 * a_sh
    if chain:
        hs = []
        hprev = h_start.reshape(1, c)
        for j in range(g):
            hj = a3[j] * hprev + b3[j]
            hs.append(hj)
            hprev = hj[SUBLANES - 1:SUBLANES, :]
        h3 = jnp.stack(hs, axis=0)
        carry_prev[...] = x[g - 1:g]
        carry_h[...] = hprev.reshape(1, 1, c)
    else:
        h3 = a3 * h0_ref[...] + b3
    h2 = h3.reshape(tt, c)
    hg_ref[...] = (h2 * gate_ref[...]).astype(hg_ref.dtype)
    for k in range(c // LANES):
        h_scr[k] = h2[:, k * LANES:(k + 1) * LANES]
        hlast_ref[:, k * LANES:(k + 1) * LANES] = h_scr[k, pl.ds(SUBLANES - 1, g, stride=SUBLANES), :]


def _rglru_core(xr, gate, prev, h0, cw, cb, wa, ba, wx, bx, lam, *, chain, tt):
    n, d = xr.shape
    c = LRU_BLOCK
    g = tt // SUBLANES
    pg = 1 if chain else g
    vec = lambda: pl.BlockSpec((1, c), lambda nb, t: (0, nb))
    return pl.pallas_call(
        functools.partial(_rglru_kernel, chain=chain),
        grid=(d // c, n // tt),
        in_specs=[pl.BlockSpec((tt, c), lambda nb, t: (t, nb)),
                  pl.BlockSpec((pg, SUBLANES, c), lambda nb, t: (0 if chain else t, 0, nb)),
                  pl.BlockSpec((pg, 1, c), lambda nb, t: (0 if chain else t, 0, nb)),
                  pl.BlockSpec((tt, c), lambda nb, t: (t, nb)),
                  pl.BlockSpec((CONV_W, c), lambda nb, t: (0, nb)),
                  vec(),
                  pl.BlockSpec((None, c, c), lambda nb, t: (nb, 0, 0)),
                  vec(),
                  pl.BlockSpec((None, c, c), lambda nb, t: (nb, 0, 0)),
                  vec(), vec()],
        out_specs=[pl.BlockSpec((tt, c), lambda nb, t: (t, nb)),
                   pl.BlockSpec((g, c), lambda nb, t: (t, nb))],
        out_shape=[jax.ShapeDtypeStruct((n, d), BF), jax.ShapeDtypeStruct((n // SUBLANES, d), F32)],
        scratch_shapes=[pltpu.VMEM((1, SUBLANES, c), F32), pltpu.VMEM((1, 1, c), F32),
                        pltpu.VMEM((c // LANES, tt, LANES), F32)],
        compiler_params=_cparams(("parallel", "arbitrary")),
    )(xr, prev, h0, gate, cw, cb, wa, ba, wx, bx, lam)


def _layer_norm(z, g, b):
    mu = jnp.mean(z, axis=-1, keepdims=True)
    zc = z - mu
    var = jnp.mean(zc * zc, axis=-1, keepdims=True)
    return zc * lax.rsqrt(var + LN_EPS) * g + b


def _mm_res_ln_kernel(a_ref, w_ref, res_ref, g_ref, b_ref, o_ref):
    y = jnp.dot(a_ref[...].astype(BF), w_ref[...], preferred_element_type=F32)
    o_ref[...] = _layer_norm(ALPHA * res_ref[...] + y, g_ref[...], b_ref[...])


def _mm_res_ln(a, w, res, g, b):
    m, k = a.shape
    d = w.shape[1]
    tm = _row_tile(m, 256)
    return pl.pallas_call(
        _mm_res_ln_kernel,
        grid=(m // tm,),
        in_specs=[pl.BlockSpec((tm, k), lambda i: (i, 0)),
                  pl.BlockSpec((k, d), lambda i: (0, 0)),
                  pl.BlockSpec((tm, d), lambda i: (i, 0)),
                  pl.BlockSpec((1, d), lambda i: (0, 0)),
                  pl.BlockSpec((1, d), lambda i: (0, 0))],
        out_specs=pl.BlockSpec((tm, d), lambda i: (i, 0)),
        out_shape=jax.ShapeDtypeStruct((m, d), F32),
        compiler_params=_cparams(("parallel",)),
    )(a, w, res, g, b)


def _route_kernel(x_ref, w_ref, b_ref, o_ref):
    logits = jnp.dot(x_ref[...], w_ref[...], preferred_element_type=F32,
                     precision=lax.Precision.HIGHEST) + b_ref[...]
    tm = logits.shape[0]
    lane = lax.broadcasted_iota(jnp.int32, (tm, LANES), 1)
    lanef = lane.astype(F32)
    big = float(LANES)
    ninf = -jnp.inf
    is_grp = (lane >= N_EXPERTS) & (lane < N_EXPERTS + N_GROUPS)
    lg = jnp.where(is_grp, logits, ninf)
    mg = jnp.max(lg, axis=-1, keepdims=True)
    p_top = 1.0 / jnp.sum(jnp.where(is_grp, jnp.exp(lg - mg), 0.0), axis=-1, keepdims=True)
    g_sel = jnp.min(jnp.where(lg == mg, lanef, big), axis=-1, keepdims=True) - float(N_EXPERTS)
    in_grp = (lane < N_EXPERTS) & ((lane >> 3).astype(F32) == g_sel)
    le = jnp.where(in_grp, logits, ninf)
    v1 = jnp.max(le, axis=-1, keepdims=True)
    j1 = jnp.min(jnp.where(le == v1, lanef, big), axis=-1, keepdims=True)
    le2 = jnp.where(lanef == j1, ninf, le)
    v2 = jnp.max(le2, axis=-1, keepdims=True)
    j2 = jnp.min(jnp.where(le2 == v2, lanef, big), axis=-1, keepdims=True)
    e2 = jnp.exp(v2 - v1)
    den = 1.0 + e2
    w1 = (1.0 / den) * p_top
    w2 = (e2 / den) * p_top
    out = jnp.where(lane == 0, j1, jnp.where(lane == 1, j2, jnp.where(lane == 2, w1, jnp.where(lane == 3, w2, 0.0))))
    o_ref[...] = out


def _route(x, w_cat, b_cat):
    m, d = x.shape
    tm = _row_tile(m, 256)
    return pl.pallas_call(
        _route_kernel,
        grid=(m // tm,),
        in_specs=[pl.BlockSpec((tm, d), lambda i: (i, 0)),
                  pl.BlockSpec((d, LANES), lambda i: (0, 0)),
                  pl.BlockSpec((1, LANES), lambda i: (0, 0))],
        out_specs=pl.BlockSpec((tm, LANES), lambda i: (i, 0)),
        out_shape=jax.ShapeDtypeStruct((m, LANES), F32),
        compiler_params=_cparams(("parallel",)),
    )(x, w_cat, b_cat)


def _moe_kernel(te_ref, ns_ref, x_ref, wt_ref, wg_ref, wu_ref, wd_ref, o_ref):
    i = pl.program_id(0)
    f = pl.program_id(1)

    @pl.when(f == 0)
    def _():
        o_ref[...] = jnp.zeros_like(o_ref)

    wg = wg_ref[...].astype(BF)
    wu = wu_ref[...].astype(BF)
    wd = wd_ref[...].astype(BF)

    def body(s, carry):
        r0 = pl.multiple_of(s * MOE_SUB, MOE_SUB)
        xs = x_ref[pl.ds(r0, MOE_SUB), :]
        hg = jnp.dot(xs, wg, preferred_element_type=F32)
        hu = jnp.dot(xs, wu, preferred_element_type=F32)
        h = (jax.nn.silu(hg) * hu) * wt_ref[pl.ds(r0, MOE_SUB), :]
        o_ref[pl.ds(r0, MOE_SUB), :] += jnp.dot(h.astype(BF), wd, preferred_element_type=F32)
        return carry

    lax.fori_loop(0, ns_ref[i], body, 0)


def _moe_ffn(tile_expert, tile_nsub, xg, wt, w_gate, w_up, w_down):
    rows, d = xg.shape
    nt = rows // MOE_TM
    ff = w_gate.shape[2]
    grid_spec = pltpu.PrefetchScalarGridSpec(
        num_scalar_prefetch=2,
        grid=(nt, ff // MOE_TF),
        in_specs=[pl.BlockSpec((MOE_TM, d), lambda i, f, te, ns: (i, 0)),
                  pl.BlockSpec((MOE_TM, 1), lambda i, f, te, ns: (i, 0)),
                  pl.BlockSpec((None, d, MOE_TF), lambda i, f, te, ns: (te[i], 0, f)),
                  pl.BlockSpec((None, d, MOE_TF), lambda i, f, te, ns: (te[i], 0, f)),
                  pl.BlockSpec((None, MOE_TF, d), lambda i, f, te, ns: (te[i], f, 0))],
        out_specs=pl.BlockSpec((MOE_TM, d), lambda i, f, te, ns: (i, 0)),
    )
    return pl.pallas_call(
        _moe_kernel,
        grid_spec=grid_spec,
        out_shape=jax.ShapeDtypeStruct((rows, d), F32),
        compiler_params=_cparams(("arbitrary", "arbitrary")),
    )(tile_expert, tile_nsub, xg, wt, w_gate, w_up, w_down)


def _ln_ple_kernel(x1_ref, y0_ref, y1_ref, g_ref, b_ref, wg_ref, p_ref, wp_ref, o_ref):
    z = ALPHA * x1_ref[...] + (y0_ref[...] + y1_ref[...])
    x2 = _layer_norm(z, g_ref[...], b_ref[...])
    gate = jax.nn.sigmoid(jnp.dot(x2.astype(BF), wg_ref[...], preferred_element_type=F32))
    proj = jnp.dot(p_ref[...].astype(BF), wp_ref[...], preferred_element_type=F32)
    o_ref[...] = x2 + gate * proj


def _ln_ple(x1, y0, y1, g, b, w_gate, p, w_proj):
    m, d = x1.shape
    pd = p.shape[1]
    tm = _row_tile(m, 256)
    row = lambda w: pl.BlockSpec((tm, w), lambda i: (i, 0))
    return pl.pallas_call(
        _ln_ple_kernel,
        grid=(m // tm,),
        in_specs=[row(d), row(d), row(d),
                  pl.BlockSpec((1, d), lambda i: (0, 0)),
                  pl.BlockSpec((1, d), lambda i: (0, 0)),
                  pl.BlockSpec((d, d), lambda i: (0, 0)),
                  row(pd),
                  pl.BlockSpec((pd, d), lambda i: (0, 0))],
        out_specs=row(d),
        out_shape=jax.ShapeDtypeStruct((m, d), F32),
        compiler_params=_cparams(("parallel",)),
    )(x1, y0, y1, g, b, w_gate, p, w_proj)


def _moe_layer(x1, w_route, b_route, w_gate, w_up, w_down):
    n, d = x1.shape
    routed = _route(x1, w_route, b_route)
    e_tok = routed[:, 0:2].astype(jnp.int32)
    w_tok = routed[:, 2:4]
    a = 2 * n
    nt = a // MOE_TM + N_EXPERTS
    e_flat = e_tok.reshape(a)
    order = jnp.argsort(e_flat, stable=True)
    counts = jnp.bincount(e_flat, length=N_EXPERTS).astype(jnp.int32)
    tiles_per = (counts + MOE_TM - 1) // MOE_TM
    tile_end = jnp.cumsum(tiles_per)
    tile_off = tile_end - tiles_per
    row_start = jnp.cumsum(counts) - counts
    e_sorted = e_flat[order]
    rank = jnp.arange(a, dtype=jnp.int32) - row_start[e_sorted]
    dest_sorted = tile_off[e_sorted] * MOE_TM + rank
    dest = jnp.zeros((a,), jnp.int32).at[order].set(dest_sorted)
    tok_pad = jnp.zeros((nt * MOE_TM,), jnp.int32).at[dest].set(jnp.arange(a, dtype=jnp.int32) // 2)
    w_pad = jnp.zeros((nt * MOE_TM,), F32).at[dest].set(w_tok.reshape(a))
    tile_id = jnp.arange(nt, dtype=jnp.int32)
    tile_expert = jnp.minimum(jnp.searchsorted(tile_end, tile_id, side="right"), N_EXPERTS - 1).astype(jnp.int32)
    valid = jnp.clip(counts[tile_expert] - (tile_id - tile_off[tile_expert]) * MOE_TM, 0, MOE_TM)
    valid = jnp.where(tile_id < tile_end[-1], valid, 0)
    tile_nsub = ((valid + MOE_SUB - 1) // MOE_SUB).astype(jnp.int32)
    last_used = tile_expert[jnp.maximum(tile_end[-1] - 1, 0)]
    tile_expert = jnp.where(tile_id < tile_end[-1], tile_expert, last_used)
    xg = jnp.take(x1.astype(BF), tok_pad, axis=0)
    yg = _moe_ffn(tile_expert, tile_nsub, xg, w_pad.reshape(-1, 1), w_gate, w_up, w_down)
    dest2 = dest.reshape(n, 2)
    return jnp.take(yg, dest2[:, 0], axis=0), jnp.take(yg, dest2[:, 1], axis=0)


def _nsa_in_kernel(x_ref, w_ref, cos_ref, sin_ref, o_ref, *, n_q_blocks):
    j = pl.program_id(1)
    u = jnp.dot(x_ref[...].astype(BF), w_ref[...], preferred_element_type=F32)
    cos = cos_ref[...]
    sin = sin_ref[...]
    heads = u.shape[1] // HEAD_DIM
    for h in range(heads):
        seg = u[:, h * HEAD_DIM:(h + 1) * HEAD_DIM]
        rot = seg * cos + pltpu.roll(seg, HEAD_DIM // 2, 1) * sin
        if h < heads // 2:
            val = rot
        else:
            val = jnp.where(j < n_q_blocks, rot, seg)
        o_ref[:, h * HEAD_DIM:(h + 1) * HEAD_DIM] = val


def _nsa_in(xb, w, cos, sin_signed):
    m, k = xb.shape
    n = w.shape[1]
    tn = 2 * N_KV * HEAD_DIM
    tm = _row_tile(m, 768)
    return pl.pallas_call(
        functools.partial(_nsa_in_kernel, n_q_blocks=(N_HEADS * HEAD_DIM) // tn),
        grid=(m // tm, n // tn),
        in_specs=[pl.BlockSpec((tm, k), lambda i, j: (i, 0)),
                  pl.BlockSpec((k, tn), lambda i, j: (0, j)),
                  pl.BlockSpec((tm, HEAD_DIM), lambda i, j: (i, 0)),
                  pl.BlockSpec((tm, HEAD_DIM), lambda i, j: (i, 0))],
        out_specs=pl.BlockSpec((tm, tn), lambda i, j: (i, j)),
        out_shape=jax.ShapeDtypeStruct((m, n), F32),
        compiler_params=_cparams(("parallel", "arbitrary")),
    )(xb, w, cos, sin_signed)


def _compress_kernel(rows_ref, w1_ref, pe_ref, w1f_ref, w2_ref, o_ref, a_ref):
    t = rows_ref.shape[0]
    nch = t // CMP_STRIDE
    for s in range(CMP_STRIDE):
        a_ref[:, s * HEAD_DIM:(s + 1) * HEAD_DIM] = rows_ref[pl.ds(s, nch, stride=CMP_STRIDE), :].astype(BF)
    p = jnp.dot(a_ref[...], w1_ref[...], preferred_element_type=F32)
    pe_term = jnp.dot(pe_ref[...].astype(BF), w1f_ref[...], preferred_element_type=F32)[0:1, :]
    pre = p[:, :HEAD_DIM] + pltpu.roll(p[:, HEAD_DIM:], nch - 1, 0) + pe_term
    o_ref[...] = jnp.dot(jax.nn.gelu(pre).astype(BF), w2_ref[...], preferred_element_type=F32).astype(o_ref.dtype)


def _compress(rows, w1cat, pe8, w1flat, w2):
    nseq, t, _ = rows.shape
    nch = t // CMP_STRIDE
    kd = CMP_STRIDE * HEAD_DIM
    return pl.pallas_call(
        _compress_kernel,
        grid=(nseq, 2, N_KV),
        in_specs=[pl.BlockSpec((None, t, HEAD_DIM), lambda b, w, g: (b, 0, w * N_KV + g)),
                  pl.BlockSpec((None, kd, 2 * HEAD_DIM), lambda b, w, g: (w, 0, 0)),
                  pl.BlockSpec((None, SUBLANES, 2 * kd), lambda b, w, g: (w, 0, 0)),
                  pl.BlockSpec((None, 2 * kd, HEAD_DIM), lambda b, w, g: (w, 0, 0)),
                  pl.BlockSpec((None, HEAD_DIM, HEAD_DIM), lambda b, w, g: (w, 0, 0))],
        out_specs=pl.BlockSpec((None, None, None, nch, HEAD_DIM), lambda b, w, g: (b, w, g, 0, 0)),
        out_shape=jax.ShapeDtypeStruct((nseq, 2, N_KV, nch, HEAD_DIM), BF),
        scratch_shapes=[pltpu.VMEM((nch, kd), BF)],
        compiler_params=_cparams(("parallel", "arbitrary", "arbitrary")),
    )(rows, w1cat, pe8, w1flat, w2)


def _stack_heads(qblk, tq):
    return jnp.concatenate([qblk[:, h * HEAD_DIM:(h + 1) * HEAD_DIM] for h in range(HPG)], axis=0)


def _cmp_attn_kernel(q_ref, kc_ref, vc_ref, oc_ref, sb_ref, *, tq, nsp, ns, nc, pos0):
    i = pl.program_id(2)
    q0 = pos0 + i * tq
    ncp = kc_ref.shape[0]
    q = _stack_heads(q_ref[...], tq).astype(BF)
    s = lax.dot_general(q, kc_ref[...], (((1,), (1,)), ((), ())), preferred_element_type=F32) * SCALE
    row = lax.broadcasted_iota(jnp.int32, (HPG * tq, ncp), 0)
    t = q0 + (row & (tq - 1))
    n = lax.broadcasted_iota(jnp.int32, (HPG * tq, ncp), 1)
    mask = (n * CMP_STRIDE + (CMP_BLOCK - 1) <= t) & (n < nc)
    sm = jnp.where(mask, s, NEG)
    m = jnp.max(sm, axis=-1, keepdims=True)
    e = jnp.where(mask, jnp.exp(sm - m), 0.0)
    l = jnp.sum(e, axis=-1, keepdims=True)
    p = e / jnp.where(l > 0.0, l, 1.0)
    oc = jnp.dot(p.astype(BF), vc_ref[...], preferred_element_type=F32)
    for h in range(HPG):
        oc_ref[:, h * HEAD_DIM:(h + 1) * HEAD_DIM] = oc[h * tq:(h + 1) * tq]
    imp = p[0:tq] + p[tq:2 * tq] + p[2 * tq:3 * tq] + p[3 * tq:4 * tq]
    nn = lax.broadcasted_iota(jnp.int32, (ncp, nsp), 0)
    bb = lax.broadcasted_iota(jnp.int32, (ncp, nsp), 1)
    dd = nn - bb * (SEL_BLOCK // CMP_STRIDE)
    ov = ((dd >= 1 - CMP_BLOCK // CMP_STRIDE) & (dd < SEL_BLOCK // CMP_STRIDE) & (nn < nc)).astype(BF)
    hi = imp.astype(BF)
    r1 = imp - hi.astype(F32)
    mid = r1.astype(BF)
    lo = (r1 - mid.astype(F32)).astype(BF)
    imp_sel = (jnp.dot(hi, ov, preferred_element_type=F32) + jnp.dot(mid, ov, preferred_element_type=F32)
               + jnp.dot(lo, ov, preferred_element_type=F32))
    bcol = lax.broadcasted_iota(jnp.int32, (tq, nsp), 1)
    bcolf = bcol.astype(F32)
    tcol = q0 + lax.broadcasted_iota(jnp.int32, (tq, nsp), 0)
    causal = bcol * SEL_BLOCK <= tcol
    forced = (bcol == (tcol >> 6)) | (bcol == 0)
    score = jnp.where(causal, imp_sel + jnp.where(forced, FORCE_BONUS, 0.0), NEG)
    score = jnp.where(bcol < ns, score, -3e38)
    sel = jnp.zeros((tq, nsp), jnp.bool_)
    for _ in range(min(N_SEL, ns)):
        mx = jnp.max(score, axis=-1, keepdims=True)
        idx = jnp.min(jnp.where(score == mx, bcolf, float(nsp)), axis=-1, keepdims=True)
        pick = bcolf == idx
        sel = sel | pick
        score = jnp.where(pick, -jnp.inf, score)
    sb_ref[...] = jnp.where(sel, 0.0, NEG)


def _cmp_attn(u5, kcvc, *, nseq, tq_total, tq, row0, t_kv, pos0):
    ncp = kcvc.shape[3]
    nc = (t_kv - CMP_BLOCK) // CMP_STRIDE + 1
    ns = -(-t_kv // SEL_BLOCK)
    nsp = -(-ns // LANES) * LANES
    nqt = tq_total // tq
    rb0 = row0 // tq
    gw = HPG * HEAD_DIM
    kern = functools.partial(_cmp_attn_kernel, tq=tq, nsp=nsp, ns=ns, nc=nc, pos0=pos0)
    return pl.pallas_call(
        kern,
        grid=(nseq, N_KV, nqt),
        in_specs=[pl.BlockSpec((tq, gw), lambda b, g, i: (rb0 + b * nqt + i, g)),
                  pl.BlockSpec((None, None, None, ncp, HEAD_DIM), lambda b, g, i: (b, 0, g, 0, 0)),
                  pl.BlockSpec((None, None, None, ncp, HEAD_DIM), lambda b, g, i: (b, 1, g, 0, 0))],
        out_specs=[pl.BlockSpec((tq, gw), lambda b, g, i: (b * nqt + i, g)),
                   pl.BlockSpec((tq, nsp), lambda b, g, i: (b * nqt + i, g))],
        out_shape=[jax.ShapeDtypeStruct((nseq * tq_total, N_KV * gw), F32),
                   jax.ShapeDtypeStruct((nseq * tq_total, N_KV * nsp), F32)],
        compiler_params=_cparams(("parallel", "arbitrary", "arbitrary")),
    )(u5, kcvc, kcvc)


def _sel_win_kernel(q_ref, sb_ref, gt_ref, oc_ref, ks_ref, vs_ref, kw_ref, vw_ref, o_ref,
                    m_ref, l_ref, acc_ref, *, tq, nsp, tk, tkw, pos0, kw_pos0):
    g = pl.program_id(1)
    i = pl.program_id(2)
    q0 = pos0 + i * tq
    rows = HPG * tq
    q = _stack_heads(q_ref[...], tq).astype(BF)
    bias = jnp.concatenate([sb_ref[...]] * HPG, axis=0).astype(BF)
    qa = jnp.concatenate([q, bias], axis=1)

    def online_update(s, v):
        m_old = m_ref[...]
        m_new = jnp.maximum(m_old, jnp.max(s, axis=-1, keepdims=True))
        alpha = jnp.exp(m_old - m_new)
        p = jnp.exp(s - m_new)
        l_ref[...] = alpha * l_ref[...] + jnp.sum(p, axis=-1, keepdims=True)
        acc_ref[...] = alpha * acc_ref[...] + jnp.dot(p.astype(BF), v, preferred_element_type=F32)
        m_ref[...] = m_new

    def reset():
        m_ref[...] = jnp.full(m_ref.shape, -jnp.inf, F32)
        l_ref[...] = jnp.zeros(l_ref.shape, F32)
        acc_ref[...] = jnp.zeros(acc_ref.shape, F32)

    reset()
    trow = q0 + (lax.broadcasted_iota(jnp.int32, (rows, tk), 0) & (tq - 1))
    kcol = lax.broadcasted_iota(jnp.int32, (rows, tk), 1)

    def sel_body(j, carry):
        k0 = pl.multiple_of(j * tk, tk)
        k = ks_ref[pl.ds(k0, tk), :]
        blk = (k0 + lax.broadcasted_iota(jnp.int32, (tk, nsp), 0)) >> 6
        onehot = (blk == lax.broadcasted_iota(jnp.int32, (tk, nsp), 1)).astype(BF)
        ka = jnp.concatenate([k, onehot], axis=1)
        s = lax.dot_general(qa, ka, (((1,), (1,)), ((), ())), preferred_element_type=F32) * SCALE
        s = jnp.where(k0 + kcol <= trow, s, NEG)
        online_update(s, vs_ref[pl.ds(k0, tk), :])
        return carry

    lax.fori_loop(0, (q0 + tq - 1) // tk + 1, sel_body, 0)
    o_s = acc_ref[...] / l_ref[...]

    reset()
    trow_w = q0 + (lax.broadcasted_iota(jnp.int32, (rows, tkw), 0) & (tq - 1))
    kcol_w = kw_pos0 + lax.broadcasted_iota(jnp.int32, (rows, tkw), 1)

    def win_body(j, carry):
        k0 = pl.multiple_of(j * tkw, tkw)
        s = lax.dot_general(q, kw_ref[pl.ds(k0, tkw), :], (((1,), (1,)), ((), ())),
                            preferred_element_type=F32) * SCALE
        kpos = k0 + kcol_w
        s = jnp.where((kpos <= trow_w) & (kpos > trow_w - WINDOW), s, NEG)
        online_update(s, vw_ref[pl.ds(k0, tkw), :])
        return carry

    j_lo = jnp.maximum(q0 - (WINDOW - 1) - kw_pos0, 0) // tkw
    j_hi = (q0 + tq - 1 - kw_pos0) // tkw
    lax.fori_loop(j_lo, j_hi + 1, win_body, 0)
    o_w = acc_ref[...] / l_ref[...]

    gts = gt_ref[...]
    oc = oc_ref[...]
    lane = lax.broadcasted_iota(jnp.int32, gts.shape, 1)
    for h in range(HPG):
        base = (g * HPG + h) * 3
        gc = jnp.sum(jnp.where(lane == base, gts, 0.0), axis=-1, keepdims=True)
        gs = jnp.sum(jnp.where(lane == base + 1, gts, 0.0), axis=-1, keepdims=True)
        gw = jnp.sum(jnp.where(lane == base + 2, gts, 0.0), axis=-1, keepdims=True)
        sl = slice(h * HEAD_DIM, (h + 1) * HEAD_DIM)
        o_ref[:, sl] = gc * oc[:, sl] + gs * o_s[h * tq:(h + 1) * tq] + gw * o_w[h * tq:(h + 1) * tq]


def _sel_win_attn(u5, sbias, gates, oc, ksvs, kwvw, *, nseq, tq_total, tq, row0, tk, tkw, pos0, kw_pos0):
    nsp = sbias.shape[1] // N_KV
    nqt = tq_total // tq
    rb0 = row0 // tq
    gw = HPG * HEAD_DIM
    tkv = ksvs.shape[1]
    tw = kwvw.shape[1]
    kern = functools.partial(_sel_win_kernel, tq=tq, nsp=nsp, tk=tk, tkw=tkw, pos0=pos0, kw_pos0=kw_pos0)
    return pl.pallas_call(
        kern,
        grid=(nseq, N_KV, nqt),
        in_specs=[pl.BlockSpec((tq, gw), lambda b, g, i: (rb0 + b * nqt + i, g)),
                  pl.BlockSpec((tq, nsp), lambda b, g, i: (b * nqt + i, g)),
                  pl.BlockSpec((tq, LANES), lambda b, g, i: (rb0 + b * nqt + i, 0)),
                  pl.BlockSpec((tq, gw), lambda b, g, i: (b * nqt + i, g)),
                  pl.BlockSpec((None, tkv, HEAD_DIM), lambda b, g, i: (b, 0, g)),
                  pl.BlockSpec((None, tkv, HEAD_DIM), lambda b, g, i: (b, 0, N_KV + g)),
                  pl.BlockSpec((None, tw, HEAD_DIM), lambda b, g, i: (b, 0, g)),
                  pl.BlockSpec((None, tw, HEAD_DIM), lambda b, g, i: (b, 0, N_KV + g))],
        out_specs=pl.BlockSpec((tq, gw), lambda b, g, i: (b * nqt + i, g)),
        out_shape=jax.ShapeDtypeStruct((nseq * tq_total, N_KV * gw), F32),
        scratch_shapes=[pltpu.VMEM((HPG * tq, 1), F32), pltpu.VMEM((HPG * tq, 1), F32),
                        pltpu.VMEM((HPG * tq, HEAD_DIM), F32)],
        compiler_params=_cparams(("parallel", "arbitrary", "arbitrary")),
    )(u5, sbias, gates, oc, ksvs, ksvs, kwvw, kwvw)


def _gelu(u):
    return jax.nn.gelu(u)


def _identity(u):
    return u


def kernel(x_prompt, x_sample, state_conv, state_h, cache_kv, cache_win, page_table, p_prompt, p_sample,
           lru_w_in, lru_conv_w, lru_conv_b, lru_w_a, lru_b_a, lru_w_x, lru_b_x, lru_lambda, lru_w_out,
           nsa_w_in, nsa_cmp_w1, nsa_cmp_pe, nsa_cmp_w2, nsa_w_out,
           ln1_g, ln1_b, ln2_g, ln2_b,
           moe_w_group, moe_b_group, moe_w_expert, moe_b_expert, moe_w_gate, moe_w_up, moe_w_down,
           ple_w_gate, ple_w_proj):
    bp, tp, d = x_prompt.shape
    bs, ts, _ = x_sample.shape
    assert bp == 1 and ts == SUBLANES
    n_p = bp * tp
    n_s = bs * ts
    n = n_p + n_s
    past_len = page_table.shape[1] * cache_kv.shape[2]
    d_rnn = lru_w_out.shape[1]
    hd = N_HEADS * HEAD_DIM
    kvw = 2 * N_KV * HEAD_DIM

    x0 = jnp.concatenate([x_prompt.reshape(n_p, d), x_sample.reshape(n_s, d)], axis=0)

    def moe_and_ple(x1, layer):
        w_route = jnp.zeros((d, LANES), F32).at[:, :N_EXPERTS].set(moe_w_expert[layer])
        w_route = w_route.at[:, N_EXPERTS:N_EXPERTS + N_GROUPS].set(moe_w_group[layer])
        b_route = jnp.zeros((1, LANES), F32).at[0, :N_EXPERTS].set(moe_b_expert[layer])
        b_route = b_route.at[0, N_EXPERTS:N_EXPERTS + N_GROUPS].set(moe_b_group[layer])
        y0, y1 = _moe_layer(x1, w_route, b_route, moe_w_gate[layer], moe_w_up[layer], moe_w_down[layer])
        p_cat = jnp.concatenate([p_prompt[layer].reshape(n_p, -1), p_sample[layer].reshape(n_s, -1)], axis=0)
        return _ln_ple(x1, y0, y1, ln2_g[layer].reshape(1, d), ln2_b[layer].reshape(1, d),
                       ple_w_gate[layer].astype(BF), p_cat, ple_w_proj[layer].astype(BF))

    w_in = lru_w_in[0].astype(BF)
    x0b = x0.astype(BF)
    gate = _mm_act(x0b, w_in[:, :d_rnn], _gelu)
    xr = _mm_act(x0b, w_in[:, d_rnn:], _identity)
    lru_args = (lru_conv_w[0], lru_conv_b[0].reshape(1, d_rnn), lru_w_a[0].astype(BF), lru_b_a[0].reshape(1, d_rnn),
                lru_w_x[0].astype(BF), lru_b_x[0].reshape(1, d_rnn), lru_lambda[0].reshape(1, d_rnn))
    hg_p, hl_p = _rglru_core(xr[:n_p], gate[:n_p], jnp.zeros((1, SUBLANES, d_rnn), F32),
                             jnp.zeros((1, 1, d_rnn), F32), *lru_args, chain=True, tt=_row_tile(n_p, 512))
    prev_s = jnp.pad(state_conv[0], ((0, 0), (SUBLANES - (CONV_W - 1), 0), (0, 0)))
    hg_s, hl_s = _rglru_core(xr[n_p:], gate[n_p:], prev_s, state_h[0].reshape(bs, 1, d_rnn),
                             *lru_args, chain=False, tt=n_s)
    hg = jnp.concatenate([hg_p, hg_s], axis=0)
    conv_prompt = xr[n_p - (CONV_W - 1):n_p].reshape(1, bp, CONV_W - 1, d_rnn)
    h_prompt = hl_p[n_p // SUBLANES - 1].reshape(1, bp, d_rnn)
    conv_sample = xr[n_p:].reshape(bs, ts, d_rnn)[:, ts - (CONV_W - 1):].reshape(1, bs, CONV_W - 1, d_rnn)
    h_sample = hl_s.reshape(1, bs, d_rnn)
    x1 = _mm_res_ln(hg, lru_w_out[0].astype(BF), x0, ln1_g[0].reshape(1, d), ln1_b[0].reshape(1, d))
    xa = moe_and_ple(x1, 0)

    w_nsa = nsa_w_in[0]
    n_main = hd + 3 * kvw
    pos = jnp.concatenate([jnp.arange(n_p, dtype=jnp.int32) % tp,
                           past_len + jnp.arange(n_s, dtype=jnp.int32) % ts])
    half = HEAD_DIM // 2
    inv = ROPE_THETA ** (-jnp.arange(half, dtype=F32) / half)
    ang = pos.astype(F32)[:, None] * inv[None, :]
    cos = jnp.concatenate([jnp.cos(ang), jnp.cos(ang)], axis=1)
    sin_signed = jnp.concatenate([-jnp.sin(ang), jnp.sin(ang)], axis=1)
    xab = xa.astype(BF)
    u5 = _nsa_in(xab, w_nsa[:, :n_main].astype(BF), cos, sin_signed)
    w_gates = jnp.zeros((d, LANES), F32).at[:, :3 * N_HEADS].set(w_nsa[:, n_main:]).astype(BF)
    gates = _mm_act(xab, w_gates, jax.nn.sigmoid)

    kv_prompt = u5[:n_p, hd:hd + 2 * kvw].reshape(1, bp, tp, 4, N_KV, HEAD_DIM)
    kv_sample = u5[n_p:, hd:hd + 2 * kvw].reshape(1, bs, ts, 4, N_KV, HEAD_DIM)
    win_new_p = u5[:n_p, hd + 2 * kvw:]
    win_new_s = u5[n_p:, hd + 2 * kvw:].reshape(bs, ts, kvw)
    win_keep_p = min(WINDOW, tp)
    win_prompt = win_new_p[n_p - win_keep_p:].reshape(1, bp, win_keep_p, 2, N_KV, HEAD_DIM)
    win_all_s = jnp.concatenate([cache_win[0].reshape(bs, -1, kvw), win_new_s], axis=1)
    win_keep_s = cache_win.shape[2]
    win_sample = win_all_s[:, -win_keep_s:].reshape(1, bs, win_keep_s, 2, N_KV, HEAD_DIM)

    kd = CMP_STRIDE * HEAD_DIM
    w1 = nsa_cmp_w1[0]
    w1flat = w1.reshape(2, 2 * kd, HEAD_DIM).astype(BF)
    w1cat = jnp.concatenate([w1flat[:, :kd], w1flat[:, kd:]], axis=2)
    pe8 = jnp.zeros((2, SUBLANES, 2 * kd), F32).at[:, 0].set(nsa_cmp_pe[0].reshape(2, 2 * kd))
    w2 = nsa_cmp_w2[0].astype(BF)

    kcvc_p = _compress(u5[:n_p, hd:hd + kvw].reshape(bp, tp, kvw), w1cat, pe8, w1flat, w2)
    tq_p = _row_tile(tp, 128)
    tk_p = _row_tile(tp, 512)
    oc_p, sb_p = _cmp_attn(u5, kcvc_p, nseq=bp, tq_total=tp, tq=tq_p, row0=0, t_kv=tp, pos0=0)
    ksvs_p = u5[:n_p, hd + kvw:hd + 2 * kvw].astype(BF).reshape(bp, tp, kvw)
    kwvw_p = win_new_p.astype(BF).reshape(bp, tp, kvw)
    o_p = _sel_win_attn(u5, sb_p, gates, oc_p, ksvs_p, kwvw_p, nseq=bp, tq_total=tp, tq=tq_p, row0=0,
                        tk=tk_p, tkw=_row_tile(tp, 128), pos0=0, kw_pos0=0)

    past = cache_kv[0][page_table].reshape(bs, past_len, 2 * kvw)
    kcvc_s = _compress(past[:, :, :kvw], w1cat, pe8, w1flat, w2)
    t_kv_s = past_len + ts
    oc_s, sb_s = _cmp_attn(u5, kcvc_s, nseq=bs, tq_total=ts, tq=ts, row0=n_p, t_kv=t_kv_s, pos0=past_len)
    tk_s = 128
    t_kv_pad = -(-t_kv_s // tk_s) * tk_s
    ksvs_s = jnp.concatenate([past[:, :, kvw:].astype(BF), kv_sample.reshape(bs, ts, 2 * kvw)[:, :, kvw:].astype(BF),
                              jnp.zeros((bs, t_kv_pad - t_kv_s, kvw), BF)], axis=1)
    t_w = win_all_s.shape[1]
    t_w_pad = -(-t_w // tk_s) * tk_s
    kwvw_s = jnp.concatenate([win_all_s.astype(BF), jnp.zeros((bs, t_w_pad - t_w, kvw), BF)], axis=1)
    o_s = _sel_win_attn(u5, sb_s, gates, oc_s, ksvs_s, kwvw_s, nseq=bs, tq_total=ts, tq=ts, row0=n_p,
                        tk=tk_s, tkw=tk_s, pos0=past_len, kw_pos0=past_len - win_keep_s)

    o_all = jnp.concatenate([o_p, o_s], axis=0)
    x1 = _mm_res_ln(o_all, nsa_w_out[0].astype(BF), xa, ln1_g[1].reshape(1, d), ln1_b[1].reshape(1, d))
    xo = moe_and_ple(x1, 1)

    y_prompt = xo[:n_p].reshape(bp, tp, d)
    y_sample = xo[n_p:].reshape(bs, ts, d)
    return (y_prompt, y_sample, conv_prompt, h_prompt, kv_prompt, win_prompt,
            conv_sample, h_sample, kv_sample, win_sample)
```

```python
import functools

import jax
import jax.numpy as jnp
from jax import lax
from jax.experimental import pallas as pl
from jax.experimental.pallas import tpu as pltpu

F32 = jnp.float32
BF = jnp.bfloat16

HEAD_DIM = 128
N_HEADS = 16
N_KV = 4
HPG = N_HEADS // N_KV
LRU_BLOCK = 256
CONV_W = 4
LRU_C = 8.0
CMP_BLOCK = 32
CMP_STRIDE = 16
SEL_BLOCK = 64
N_SEL = 16
WINDOW = 512
ROPE_THETA = 10000.0
N_GROUPS = 4
EPG = 8
N_EXPERTS = N_GROUPS * EPG
DEPTH = 2
ALPHA = (2 * DEPTH) ** 0.25
LN_EPS = 1e-5
NEG = -1e30
FORCE_BONUS = 1e4
SCALE = HEAD_DIM ** -0.5
SEL_SHIFT = SEL_BLOCK.bit_length() - 1
EPG_SHIFT = EPG.bit_length() - 1

LANES = 128
SUBLANES = 8
VMEM_LIMIT = 56 * 1024 * 1024
MOE_TM = 1024
MOE_SUB = 256
MOE_TF = 256


def _cparams(sem):
    return pltpu.CompilerParams(dimension_semantics=sem, vmem_limit_bytes=VMEM_LIMIT)


def _row_tile(n, cap):
    best = None
    for t in range(SUBLANES, cap + 1, SUBLANES):
        if n % t == 0:
            best = t
    assert best is not None, (n, cap)
    return best


def _mm_act_kernel(x_ref, w_ref, o_ref, *, act):
    u = jnp.dot(x_ref[...].astype(BF), w_ref[...], preferred_element_type=F32)
    o_ref[...] = act(u).astype(o_ref.dtype)


def _mm_act(x, w, act, out_dtype=F32, tn=512):
    m, k = x.shape
    n = w.shape[1]
    tm = _row_tile(m, 768)
    tn = min(tn, n)
    assert n % tn == 0
    return pl.pallas_call(
        functools.partial(_mm_act_kernel, act=act),
        grid=(m // tm, n // tn),
        in_specs=[pl.BlockSpec((tm, k), lambda i, j: (i, 0)),
                  pl.BlockSpec((k, tn), lambda i, j: (0, j))],
        out_specs=pl.BlockSpec((tm, tn), lambda i, j: (i, j)),
        out_shape=jax.ShapeDtypeStruct((m, n), out_dtype),
        compiler_params=_cparams(("parallel", "arbitrary")),
        name="mm_act",
    )(x, w)


def _rglru_kernel(xr_ref, prev_ref, h0_ref, gate_ref, cw_ref, cb_ref, wa_ref, ba_ref, wx_ref, bx_ref,
                  lam_ref, hg_ref, hlast_ref, carry_prev, carry_h, h_scr, *, chain):
    tt, c = xr_ref.shape
    g = tt // SUBLANES
    x = xr_ref[...].reshape(g, SUBLANES, c)
    if chain:
        first = pl.program_id(1) == 0
        p0 = jnp.where(first, prev_ref[...], carry_prev[...])
        prev = jnp.concatenate([p0, x[:-1]], axis=0) if g > 1 else p0
        h_start = jnp.where(first, h0_ref[...], carry_h[...])
    else:
        prev = prev_ref[...]
    sub = lax.broadcasted_iota(jnp.int32, (g, SUBLANES, c), 1)
    cw = [cw_ref[k:k + 1, :].reshape(1, 1, c) for k in range(CONV_W)]
    xc = cb_ref[...].reshape(1, 1, c) + x * cw[CONV_W - 1]
    for s in range(1, CONV_W):
        sh = jnp.where(sub < s, pltpu.roll(prev, s, 1), pltpu.roll(x, s, 1))
        xc = xc + sh * cw[CONV_W - 1 - s]
    xc2 = xc.reshape(tt, c)
    xb = xc2.astype(BF)
    r = jax.nn.sigmoid(jnp.dot(xb, wa_ref[...], preferred_element_type=F32) + ba_ref[...])
    i = jax.nn.sigmoid(jnp.dot(xb, wx_ref[...], preferred_element_type=F32) + bx_ref[...])
    log_a = (-LRU_C * r) * jax.nn.softplus(-lam_ref[...])
    a = jnp.exp(log_a)
    bterm = jnp.sqrt(-jnp.tanh(log_a) * (a * a + 1.0)) * (i * xc2)
    a3 = a.reshape(g, SUBLANES, c)
    b3 = bterm.reshape(g, SUBLANES, c)
    for d in (1, 2, 4):
        a_sh = jnp.where(sub < d, 1.0, pltpu.roll(a3, d, 1))
        b_sh = jnp.where(sub < d, 0.0, pltpu.roll(b3, d, 1))
        b3 = a3 * b_sh + b3
        a3 = a3 * a_sh
    if chain:
        hs = []
        hprev = h_start.reshape(1, c)
        for j in range(g):
            hj = a3[j] * hprev + b3[j]
            hs.append(hj)
            hprev = hj[SUBLANES - 1:SUBLANES, :]
        h3 = jnp.stack(hs, axis=0)
        carry_prev[...] = x[g - 1:g]
        carry_h[...] = hprev.reshape(1, 1, c)
    else:
        h3 = a3 * h0_ref[...] + b3
    h2 = h3.reshape(tt, c)
    hg_ref[...] = (h2 * gate_ref[...]).astype(hg_ref.dtype)
    for k in range(c // LANES):
        h_scr[k] = h2[:, k * LANES:(k + 1) * LANES]
        hlast_ref[:, k * LANES:(k + 1) * LANES] = h_scr[k, pl.ds(SUBLANES - 1, g, stride=SUBLANES), :]


def _rglru_core(xr, gate, prev, h0, cw, cb, wa, ba, wx, bx, lam, *, chain, tt):
    n, d = xr.shape
    c = LRU_BLOCK
    g = tt // SUBLANES
    pg = 1 if chain else g
    vec = lambda: pl.BlockSpec((1, c), lambda nb, t: (0, nb))
    return pl.pallas_call(
        functools.partial(_rglru_kernel, chain=chain),
        grid=(d // c, n // tt),
        in_specs=[pl.BlockSpec((tt, c), lambda nb, t: (t, nb)),
                  pl.BlockSpec((pg, SUBLANES, c), lambda nb, t: (0 if chain else t, 0, nb)),
                  pl.BlockSpec((pg, 1, c), lambda nb, t: (0 if chain else t, 0, nb)),
                  pl.BlockSpec((tt, c), lambda nb, t: (t, nb)),
                  pl.BlockSpec((CONV_W, c), lambda nb, t: (0, nb)),
                  vec(),
                  pl.BlockSpec((None, c, c), lambda nb, t: (nb, 0, 0)),
                  vec(),
                  pl.BlockSpec((None, c, c), lambda nb, t: (nb, 0, 0)),
                  vec(), vec()],
        out_specs=[pl.BlockSpec((tt, c), lambda nb, t: (t, nb)),
                   pl.BlockSpec((g, c), lambda nb, t: (t, nb))],
        out_shape=[jax.ShapeDtypeStruct((n, d), BF), jax.ShapeDtypeStruct((n // SUBLANES, d), F32)],
        scratch_shapes=[pltpu.VMEM((1, SUBLANES, c), F32), pltpu.VMEM((1, 1, c), F32),
                        pltpu.VMEM((c // LANES, tt, LANES), F32)],
        compiler_params=_cparams(("parallel", "arbitrary")),
        name="rglru_core",
    )(xr, prev, h0, gate, cw, cb, wa, ba, wx, bx, lam)


def _layer_norm(z, g, b):
    mu = jnp.mean(z, axis=-1, keepdims=True)
    zc = z - mu
    var = jnp.mean(zc * zc, axis=-1, keepdims=True)
    return zc * lax.rsqrt(var + LN_EPS) * g + b


def _mm_res_ln_kernel(a_ref, w_ref, res_ref, g_ref, b_ref, o_ref):
    y = jnp.dot(a_ref[...].astype(BF), w_ref[...], preferred_element_type=F32)
    o_ref[...] = _layer_norm(ALPHA * res_ref[...] + y, g_ref[...], b_ref[...])


def _mm_res_ln(a, w, res, g, b):
    m, k = a.shape
    d = w.shape[1]
    tm = _row_tile(m, 256)
    return pl.pallas_call(
        _mm_res_ln_kernel,
        grid=(m // tm,),
        in_specs=[pl.BlockSpec((tm, k), lambda i: (i, 0)),
                  pl.BlockSpec((k, d), lambda i: (0, 0)),
                  pl.BlockSpec((tm, d), lambda i: (i, 0)),
                  pl.BlockSpec((1, d), lambda i: (0, 0)),
                  pl.BlockSpec((1, d), lambda i: (0, 0))],
        out_specs=pl.BlockSpec((tm, d), lambda i: (i, 0)),
        out_shape=jax.ShapeDtypeStruct((m, d), F32),
        compiler_params=_cparams(("parallel",)),
        name="mm_res_ln",
    )(a, w, res, g, b)


def _route_kernel(x_ref, w_ref, b_ref, o_ref):
    logits = jnp.dot(x_ref[...], w_ref[...], preferred_element_type=F32,
                     precision=lax.Precision.HIGHEST) + b_ref[...]
    tm = logits.shape[0]
    lane = lax.broadcasted_iota(jnp.int32, (tm, LANES), 1)
    lanef = lane.astype(F32)
    big = float(LANES)
    ninf = -jnp.inf
    is_grp = (lane >= N_EXPERTS) & (lane < N_EXPERTS + N_GROUPS)
    lg = jnp.where(is_grp, logits, ninf)
    mg = jnp.max(lg, axis=-1, keepdims=True)
    p_top = 1.0 / jnp.sum(jnp.where(is_grp, jnp.exp(lg - mg), 0.0), axis=-1, keepdims=True)
    g_sel = jnp.min(jnp.where(lg == mg, lanef, big), axis=-1, keepdims=True) - float(N_EXPERTS)
    in_grp = (lane < N_EXPERTS) & ((lane >> EPG_SHIFT).astype(F32) == g_sel)
    le = jnp.where(in_grp, logits, ninf)
    v1 = jnp.max(le, axis=-1, keepdims=True)
    j1 = jnp.min(jnp.where(le == v1, lanef, big), axis=-1, keepdims=True)
    le2 = jnp.where(lanef == j1, ninf, le)
    v2 = jnp.max(le2, axis=-1, keepdims=True)
    j2 = jnp.min(jnp.where(le2 == v2, lanef, big), axis=-1, keepdims=True)
    e2 = jnp.exp(v2 - v1)
    den = 1.0 + e2
    w1 = (1.0 / den) * p_top
    w2 = (e2 / den) * p_top
    out = jnp.where(lane == 0, j1, jnp.where(lane == 1, j2, jnp.where(lane == 2, w1, jnp.where(lane == 3, w2, 0.0))))
    o_ref[...] = out


def _route(x, w_cat, b_cat):
    m, d = x.shape
    tm = _row_tile(m, 256)
    return pl.pallas_call(
        _route_kernel,
        grid=(m // tm,),
        in_specs=[pl.BlockSpec((tm, d), lambda i: (i, 0)),
                  pl.BlockSpec((d, LANES), lambda i: (0, 0)),
                  pl.BlockSpec((1, LANES), lambda i: (0, 0))],
        out_specs=pl.BlockSpec((tm, LANES), lambda i: (i, 0)),
        out_shape=jax.ShapeDtypeStruct((m, LANES), F32),
        compiler_params=_cparams(("parallel",)),
        name="moe_route",
    )(x, w_cat, b_cat)


def _moe_kernel(te_ref, ns_ref, st_ref, x_ref, wt_ref, wg_ref, wu_ref, wd_ref, o_ref):
    i = pl.program_id(0)
    f = pl.program_id(1)

    @pl.when(f == 0)
    def _():
        o_ref[...] = jnp.zeros_like(o_ref)

    wg = wg_ref[...].astype(BF)
    wu = wu_ref[...].astype(BF)
    wd = wd_ref[...].astype(BF)

    def body(s, carry):
        r0 = pl.multiple_of(s * MOE_SUB, MOE_SUB)
        xs = x_ref[pl.ds(r0, MOE_SUB), :].astype(BF)
        hg = jnp.dot(xs, wg, preferred_element_type=F32)
        hu = jnp.dot(xs, wu, preferred_element_type=F32)
        h = (jax.nn.silu(hg) * hu) * wt_ref[pl.ds(r0, MOE_SUB), :]
        o_ref[pl.ds(r0, MOE_SUB), :] += jnp.dot(h.astype(BF), wd, preferred_element_type=F32)
        return carry

    lax.fori_loop(0, ns_ref[i], body, 0)


def _moe_ffn(tile_expert, tile_nsub, tile_start, xg, wt, w_gate, w_up, w_down, layer):
    _, d = xg.shape
    nt = tile_expert.shape[0]
    ff = w_gate.shape[3]
    grid_spec = pltpu.PrefetchScalarGridSpec(
        num_scalar_prefetch=3,
        grid=(nt, ff // MOE_TF),
        in_specs=[pl.BlockSpec((pl.Element(MOE_TM), pl.Element(d)), lambda i, f, te, ns, st: (st[i] * MOE_SUB, 0)),
                  pl.BlockSpec((pl.Element(MOE_TM), pl.Element(1)), lambda i, f, te, ns, st: (st[i] * MOE_SUB, 0)),
                  pl.BlockSpec((None, None, d, MOE_TF), lambda i, f, te, ns, st: (layer, te[i], 0, f)),
                  pl.BlockSpec((None, None, d, MOE_TF), lambda i, f, te, ns, st: (layer, te[i], 0, f)),
                  pl.BlockSpec((None, None, MOE_TF, d), lambda i, f, te, ns, st: (layer, te[i], f, 0))],
        out_specs=pl.BlockSpec((MOE_TM, d), lambda i, f, te, ns, st: (i, 0)),
    )
    return pl.pallas_call(
        _moe_kernel,
        grid_spec=grid_spec,
        out_shape=jax.ShapeDtypeStruct((nt * MOE_TM, d), F32),
        compiler_params=_cparams(("arbitrary", "arbitrary")),
        name="moe_ffn",
    )(tile_expert, tile_nsub, tile_start, xg, wt, w_gate, w_up, w_down)


def _ln_ple_kernel(x1_ref, y0_ref, y1_ref, g_ref, b_ref, wg_ref, p_ref, wp_ref, o_ref):
    z = ALPHA * x1_ref[...] + (y0_ref[...] + y1_ref[...])
    x2 = _layer_norm(z, g_ref[...], b_ref[...])
    gate = jax.nn.sigmoid(jnp.dot(x2.astype(BF), wg_ref[...], preferred_element_type=F32))
    proj = jnp.dot(p_ref[...].astype(BF), wp_ref[...], preferred_element_type=F32)
    o_ref[...] = x2 + gate * proj


def _ln_ple(x1, y0, y1, g, b, w_gate, p, w_proj):
    m, d = x1.shape
    pd = p.shape[1]
    tm = _row_tile(m, 256)
    row = lambda w: pl.BlockSpec((tm, w), lambda i: (i, 0))
    return pl.pallas_call(
        _ln_ple_kernel,
        grid=(m // tm,),
        in_specs=[row(d), row(d), row(d),
                  pl.BlockSpec((1, d), lambda i: (0, 0)),
                  pl.BlockSpec((1, d), lambda i: (0, 0)),
                  pl.BlockSpec((d, d), lambda i: (0, 0)),
                  row(pd),
                  pl.BlockSpec((pd, d), lambda i: (0, 0))],
        out_specs=row(d),
        out_shape=jax.ShapeDtypeStruct((m, d), F32),
        compiler_params=_cparams(("parallel",)),
        name="ln_ple",
    )(x1, y0, y1, g, b, w_gate, p, w_proj)


def _moe_layer(x1, w_route, b_route, w_gate, w_up, w_down, layer):
    n, d = x1.shape
    routed = _route(x1, w_route, b_route)
    e_tok = routed[:, 0:2].astype(jnp.int32)
    w_tok = routed[:, 2:4]
    a = 2 * n
    sub_per_tile = MOE_TM // MOE_SUB
    nt = a // MOE_TM + N_EXPERTS
    n_sub_max = a // MOE_SUB + N_EXPERTS
    e_flat = e_tok.reshape(a)
    onehot = (e_flat[:, None] == jnp.arange(N_EXPERTS, dtype=jnp.int32)[None, :]).astype(jnp.int32)
    csum = jnp.cumsum(onehot, axis=0)
    counts = csum[a - 1]
    rank = jnp.sum(csum * onehot, axis=1) - 1
    sub_per = (counts + MOE_SUB - 1) // MOE_SUB
    sub_off = jnp.cumsum(sub_per) - sub_per
    tiles_per = (counts + MOE_TM - 1) // MOE_TM
    tile_end = jnp.cumsum(tiles_per)
    tile_off = tile_end - tiles_per
    src_row = sub_off[e_flat] * MOE_SUB + rank
    dst_row = (tile_off[e_flat] + rank // MOE_TM) * MOE_TM + rank % MOE_TM
    rows_in = (n_sub_max + sub_per_tile) * MOE_SUB
    tok_pad = jnp.zeros((rows_in,), jnp.int32).at[src_row].set(jnp.arange(a, dtype=jnp.int32) // 2)
    w_pad = jnp.zeros((rows_in,), F32).at[src_row].set(w_tok.reshape(a))
    tile_id = jnp.arange(nt, dtype=jnp.int32)
    n_used = tile_end[N_EXPERTS - 1]
    tile_expert = jnp.minimum(jnp.sum((tile_end[None, :] <= tile_id[:, None]).astype(jnp.int32), axis=1),
                              N_EXPERTS - 1)
    k_in_expert = tile_id - tile_off[tile_expert]
    used = tile_id < n_used
    tile_nsub = jnp.where(used, jnp.clip(sub_per[tile_expert] - k_in_expert * sub_per_tile, 0, sub_per_tile), 0)
    tile_start = jnp.where(used, sub_off[tile_expert] + k_in_expert * sub_per_tile, 0)
    last_used = jnp.sum(jnp.where(tile_id == n_used - 1, tile_expert, 0))
    tile_expert = jnp.where(used, tile_expert, last_used)
    xg = jnp.take(x1, tok_pad, axis=0)
    yg = _moe_ffn(tile_expert.astype(jnp.int32), tile_nsub.astype(jnp.int32), tile_start.astype(jnp.int32),
                  xg, w_pad.reshape(-1, 1), w_gate, w_up, w_down, layer)
    dst2 = dst_row.reshape(n, 2)
    return jnp.take(yg, dst2[:, 0], axis=0), jnp.take(yg, dst2[:, 1], axis=0)


def _nsa_in_kernel(x_ref, w_ref, cos_ref, sin_ref, o_ref, *, n_q_blocks):
    j = pl.program_id(1)
    u = jnp.dot(x_ref[...].astype(BF), w_ref[...], preferred_element_type=F32)
    cos = cos_ref[...]
    sin = sin_ref[...]
    heads = u.shape[1] // HEAD_DIM
    for h in range(heads):
        seg = u[:, h * HEAD_DIM:(h + 1) * HEAD_DIM]
        rot = seg * cos + pltpu.roll(seg, HEAD_DIM // 2, 1) * sin
        if h < heads // 2:
            val = rot
        else:
            val = jnp.where(j < n_q_blocks, rot, seg)
        o_ref[:, h * HEAD_DIM:(h + 1) * HEAD_DIM] = val


def _nsa_in(xb, w, cos, sin_signed):
    m, k = xb.shape
    n = w.shape[1]
    tn = 2 * N_KV * HEAD_DIM
    tm = _row_tile(m, 768)
    return pl.pallas_call(
        functools.partial(_nsa_in_kernel, n_q_blocks=(N_HEADS * HEAD_DIM) // tn),
        grid=(m // tm, n // tn),
        in_specs=[pl.BlockSpec((tm, k), lambda i, j: (i, 0)),
                  pl.BlockSpec((k, tn), lambda i, j: (0, j)),
                  pl.BlockSpec((tm, HEAD_DIM), lambda i, j: (i, 0)),
                  pl.BlockSpec((tm, HEAD_DIM), lambda i, j: (i, 0))],
        out_specs=pl.BlockSpec((tm, tn), lambda i, j: (i, j)),
        out_shape=jax.ShapeDtypeStruct((m, n), F32),
        compiler_params=_cparams(("parallel", "arbitrary")),
        name="nsa_in",
    )(xb, w, cos, sin_signed)


def _compress_kernel(rows_ref, w1_ref, pe_ref, w1f_ref, w2_ref, o_ref, a_ref):
    t = rows_ref.shape[0]
    nch = t // CMP_STRIDE
    for s in range(CMP_STRIDE):
        a_ref[:, s * HEAD_DIM:(s + 1) * HEAD_DIM] = rows_ref[pl.ds(s, nch, stride=CMP_STRIDE), :].astype(BF)
    p = jnp.dot(a_ref[...], w1_ref[...], preferred_element_type=F32)
    pe_term = jnp.dot(pe_ref[...].astype(BF), w1f_ref[...], preferred_element_type=F32)[0:1, :]
    pre = p[:, :HEAD_DIM] + pltpu.roll(p[:, HEAD_DIM:], nch - 1, 0) + pe_term
    o_ref[...] = jnp.dot(jax.nn.gelu(pre).astype(BF), w2_ref[...], preferred_element_type=F32).astype(o_ref.dtype)


def _compress(rows, w1cat, pe8, w1flat, w2):
    nseq, t, _ = rows.shape
    nch = t // CMP_STRIDE
    kd = CMP_STRIDE * HEAD_DIM
    return pl.pallas_call(
        _compress_kernel,
        grid=(nseq, 2, N_KV),
        in_specs=[pl.BlockSpec((None, t, HEAD_DIM), lambda b, w, g: (b, 0, w * N_KV + g)),
                  pl.BlockSpec((None, kd, 2 * HEAD_DIM), lambda b, w, g: (w, 0, 0)),
                  pl.BlockSpec((None, SUBLANES, 2 * kd), lambda b, w, g: (w, 0, 0)),
                  pl.BlockSpec((None, 2 * kd, HEAD_DIM), lambda b, w, g: (w, 0, 0)),
                  pl.BlockSpec((None, HEAD_DIM, HEAD_DIM), lambda b, w, g: (w, 0, 0))],
        out_specs=pl.BlockSpec((None, None, None, nch, HEAD_DIM), lambda b, w, g: (b, w, g, 0, 0)),
        out_shape=jax.ShapeDtypeStruct((nseq, 2, N_KV, nch, HEAD_DIM), BF),
        scratch_shapes=[pltpu.VMEM((nch, kd), BF)],
        compiler_params=_cparams(("parallel", "arbitrary", "arbitrary")),
        name="compress",
    )(rows, w1cat, pe8, w1flat, w2)


def _stack_heads(qblk, tq):
    return jnp.concatenate([qblk[:, h * HEAD_DIM:(h + 1) * HEAD_DIM] for h in range(HPG)], axis=0)


def _cmp_attn_kernel(q_ref, kc_ref, vc_ref, oc_ref, sb_ref, *, tq, nsp, ns, nc, pos0):
    q0 = pos0 + pl.program_id(2) * tq
    _cmp_attn_body(q_ref, kc_ref[...], vc_ref[...], oc_ref, sb_ref, q0, tq=tq, nsp=nsp, ns=ns, nc=nc)


def _cmp_attn_body(q_ref, kc, vc, oc_ref, sb_ref, q0, *, tq, nsp, ns, nc):
    ncp = kc.shape[0]
    q = _stack_heads(q_ref[...], tq).astype(BF)
    s = lax.dot_general(q, kc, (((1,), (1,)), ((), ())), preferred_element_type=F32) * SCALE
    row = lax.broadcasted_iota(jnp.int32, (HPG * tq, ncp), 0)
    t = q0 + (row & (tq - 1))
    n = lax.broadcasted_iota(jnp.int32, (HPG * tq, ncp), 1)
    mask = (n * CMP_STRIDE + (CMP_BLOCK - 1) <= t) & (n < nc)
    sm = jnp.where(mask, s, NEG)
    m = jnp.max(sm, axis=-1, keepdims=True)
    e = jnp.where(mask, jnp.exp(sm - m), 0.0)
    l = jnp.sum(e, axis=-1, keepdims=True)
    p = e / jnp.where(l > 0.0, l, 1.0)
    oc = jnp.dot(p.astype(BF), vc, preferred_element_type=F32)
    for h in range(HPG):
        oc_ref[:, h * HEAD_DIM:(h + 1) * HEAD_DIM] = oc[h * tq:(h + 1) * tq]
    imp = p[0:tq] + p[tq:2 * tq] + p[2 * tq:3 * tq] + p[3 * tq:4 * tq]
    nn = lax.broadcasted_iota(jnp.int32, (ncp, nsp), 0)
    bb = lax.broadcasted_iota(jnp.int32, (ncp, nsp), 1)
    dd = nn - bb * (SEL_BLOCK // CMP_STRIDE)
    ov = ((dd >= 1 - CMP_BLOCK // CMP_STRIDE) & (dd < SEL_BLOCK // CMP_STRIDE) & (nn < nc)).astype(BF)
    hi = imp.astype(BF)
    r1 = imp - hi.astype(F32)
    mid = r1.astype(BF)
    lo = (r1 - mid.astype(F32)).astype(BF)
    imp_sel = (jnp.dot(hi, ov, preferred_element_type=F32) + jnp.dot(mid, ov, preferred_element_type=F32)
               + jnp.dot(lo, ov, preferred_element_type=F32))
    bcol = lax.broadcasted_iota(jnp.int32, (tq, nsp), 1)
    bcolf = bcol.astype(F32)
    tcol = q0 + lax.broadcasted_iota(jnp.int32, (tq, nsp), 0)
    causal = bcol * SEL_BLOCK <= tcol
    forced = (bcol == (tcol >> SEL_SHIFT)) | (bcol == 0)
    score = jnp.where(causal, imp_sel + jnp.where(forced, FORCE_BONUS, 0.0), NEG)
    score = jnp.where(bcol < ns, score, -3e38)
    sel = jnp.zeros((tq, nsp), jnp.bool_)
    for _ in range(min(N_SEL, ns)):
        mx = jnp.max(score, axis=-1, keepdims=True)
        idx = jnp.min(jnp.where(score == mx, bcolf, float(nsp)), axis=-1, keepdims=True)
        pick = bcolf == idx
        sel = sel | pick
        score = jnp.where(pick, -jnp.inf, score)
    sb_ref[...] = jnp.where(sel, 0.0, NEG)


def _cmp_attn(u5, kcvc, *, nseq, tq_total, tq, row0, t_kv, pos0):
    ncp = kcvc.shape[3]
    nc = (t_kv - CMP_BLOCK) // CMP_STRIDE + 1
    ns = -(-t_kv // SEL_BLOCK)
    nsp = -(-ns // LANES) * LANES
    nqt = tq_total // tq
    rb0 = row0 // tq
    gw = HPG * HEAD_DIM
    kern = functools.partial(_cmp_attn_kernel, tq=tq, nsp=nsp, ns=ns, nc=nc, pos0=pos0)
    return pl.pallas_call(
        kern,
        grid=(nseq, N_KV, nqt),
        in_specs=[pl.BlockSpec((tq, gw), lambda b, g, i: (rb0 + b * nqt + i, g)),
                  pl.BlockSpec((None, None, None, ncp, HEAD_DIM), lambda b, g, i: (b, 0, g, 0, 0)),
                  pl.BlockSpec((None, None, None, ncp, HEAD_DIM), lambda b, g, i: (b, 1, g, 0, 0))],
        out_specs=[pl.BlockSpec((tq, gw), lambda b, g, i: (b * nqt + i, g)),
                   pl.BlockSpec((tq, nsp), lambda b, g, i: (b * nqt + i, g))],
        out_shape=[jax.ShapeDtypeStruct((nseq * tq_total, N_KV * gw), F32),
                   jax.ShapeDtypeStruct((nseq * tq_total, N_KV * nsp), F32)],
        compiler_params=_cparams(("parallel", "arbitrary", "arbitrary")),
        name="cmp_attn",
    )(u5, kcvc, kcvc)


EXP2_SCALE = SCALE * 1.4426950408889634


def _sel_win_kernel(q_ref, sb_ref, gt_ref, oc_ref, ks_ref, vs_ref, kw_ref, vw_ref, o_ref,
                    m_ref, acc_ref, s_ref, *, tq, nsp, tk):
    g = pl.program_id(1)
    q0 = pl.program_id(2) * tq
    rows = HPG * tq
    nt = (((1,), (1,)), ((), ()))
    q = _stack_heads(q_ref[...], tq).astype(BF)
    bias = jnp.concatenate([sb_ref[...]] * HPG, axis=0).astype(BF)
    qa = jnp.concatenate([q, bias], axis=1)
    ones = jnp.ones((tk, HEAD_DIM), BF)

    m_ref[...] = jnp.full(m_ref.shape, -jnp.inf, F32)
    acc_ref[...] = jnp.zeros(acc_ref.shape, F32)

    def scores(j):
        k0 = pl.multiple_of(j * tk, tk)
        blk = (k0 + lax.broadcasted_iota(jnp.int32, (tk, nsp), 0)) >> SEL_SHIFT
        onehot = (blk == lax.broadcasted_iota(jnp.int32, (tk, nsp), 1)).astype(BF)
        ka = jnp.concatenate([ks_ref[pl.ds(k0, tk), :], onehot], axis=1)
        return [lax.dot_general(qa[h * tq:(h + 1) * tq], ka, nt, preferred_element_type=F32) for h in range(HPG)]

    def consume(ss, j, masked):
        k0 = pl.multiple_of(j * tk, tk)
        va = jnp.concatenate([vs_ref[pl.ds(k0, tk), :], ones], axis=1)
        if masked:
            ok = (k0 + lax.broadcasted_iota(jnp.int32, (tq, tk), 1)
                  <= q0 + lax.broadcasted_iota(jnp.int32, (tq, tk), 0))
            ss = [jnp.where(ok, sh, NEG) for sh in ss]
        ps, alphas = [], []
        for h in range(HPG):
            rs = slice(h * tq, (h + 1) * tq)
            m_old = m_ref[rs]
            m_new = jnp.maximum(m_old, jnp.max(ss[h], axis=-1, keepdims=True))
            alphas.append(jnp.exp2((m_old - m_new) * EXP2_SCALE))
            ps.append(jnp.exp2((ss[h] - m_new) * EXP2_SCALE).astype(BF))
            m_ref[rs] = m_new
        for h in range(HPG):
            rs = slice(h * tq, (h + 1) * tq)
            acc_ref[rs] = alphas[h] * acc_ref[rs] + jnp.dot(ps[h], va, preferred_element_type=F32)

    j_last = (q0 + tq - 1) // tk
    for h, sh in enumerate(scores(0)):
        s_ref[h] = sh

    def sel_body(j, carry):
        cur = [s_ref[h] for h in range(HPG)]
        nxt = scores(j + 1)
        consume(cur, j, False)
        for h in range(HPG):
            s_ref[h] = nxt[h]
        return carry

    lax.fori_loop(0, j_last, sel_body, 0)
    nkw = WINDOW + tq
    w0 = pl.multiple_of(jnp.maximum(q0 - WINDOW, 0), tq)
    sw = lax.dot_general(q, kw_ref[pl.ds(w0, nkw), :], nt, preferred_element_type=F32)
    consume([s_ref[h] for h in range(HPG)], j_last, True)
    acc = acc_ref[...]
    o_s = acc[:, :HEAD_DIM] / acc[:, HEAD_DIM:HEAD_DIM + 1]
    vwa = jnp.concatenate([vw_ref[pl.ds(w0, nkw), :], jnp.ones((nkw, HEAD_DIM), BF)], axis=1)
    t = q0 + (lax.broadcasted_iota(jnp.int32, (rows, nkw), 0) & (tq - 1))
    kpos = w0 + lax.broadcasted_iota(jnp.int32, (rows, nkw), 1)
    sw = jnp.where((kpos <= t) & (kpos > t - WINDOW), sw, NEG)
    pw = jnp.exp2((sw - jnp.max(sw, axis=-1, keepdims=True)) * EXP2_SCALE)
    aw = jnp.dot(pw.astype(BF), vwa, preferred_element_type=F32)
    o_w = aw[:, :HEAD_DIM] / aw[:, HEAD_DIM:HEAD_DIM + 1]

    gts = gt_ref[...]
    oc = oc_ref[...]
    lane = lax.broadcasted_iota(jnp.int32, gts.shape, 1)
    for h in range(HPG):
        base = (g * HPG + h) * 3
        gc = jnp.sum(jnp.where(lane == base, gts, 0.0), axis=-1, keepdims=True)
        gs = jnp.sum(jnp.where(lane == base + 1, gts, 0.0), axis=-1, keepdims=True)
        gw = jnp.sum(jnp.where(lane == base + 2, gts, 0.0), axis=-1, keepdims=True)
        sl = slice(h * HEAD_DIM, (h + 1) * HEAD_DIM)
        rs = slice(h * tq, (h + 1) * tq)
        o_ref[:, sl] = gc * oc[:, sl] + gs * o_s[rs] + gw * o_w[rs]


def _sel_win_attn(u5, sbias, gates, oc, ksvs, kwvw, *, tq, tk):
    t = ksvs.shape[0]
    nsp = sbias.shape[1] // N_KV
    nqt = t // tq
    gw = HPG * HEAD_DIM
    assert t % tk == 0 and t % tq == 0 and t >= WINDOW + tq and WINDOW % tq == 0
    kern = functools.partial(_sel_win_kernel, tq=tq, nsp=nsp, tk=tk)
    return pl.pallas_call(
        kern,
        grid=(1, N_KV, nqt),
        in_specs=[pl.BlockSpec((tq, gw), lambda b, g, i: (i, g)),
                  pl.BlockSpec((tq, nsp), lambda b, g, i: (i, g)),
                  pl.BlockSpec((tq, LANES), lambda b, g, i: (i, 0)),
                  pl.BlockSpec((tq, gw), lambda b, g, i: (i, g)),
                  pl.BlockSpec((t, HEAD_DIM), lambda b, g, i: (0, g)),
                  pl.BlockSpec((t, HEAD_DIM), lambda b, g, i: (0, N_KV + g)),
                  pl.BlockSpec((t, HEAD_DIM), lambda b, g, i: (0, g)),
                  pl.BlockSpec((t, HEAD_DIM), lambda b, g, i: (0, N_KV + g))],
        out_specs=pl.BlockSpec((tq, gw), lambda b, g, i: (i, g)),
        out_shape=jax.ShapeDtypeStruct((t, N_KV * gw), F32),
        scratch_shapes=[pltpu.VMEM((HPG * tq, 1), F32), pltpu.VMEM((HPG * tq, 2 * HEAD_DIM), F32),
                        pltpu.VMEM((HPG, tq, tk), F32)],
        compiler_params=_cparams(("parallel", "arbitrary", "arbitrary")),
        name="sel_win_prompt",
    )(u5, sbias, gates, oc, ksvs, ksvs, kwvw, kwvw)


ROWS_PER_TOKEN = 4 * N_KV
CMP_PAGES_PER_STEP = 8


def _compress_paged_kernel(pt_ref, *refs, pg, page_size):
    page_refs = refs[:pg]
    w1_ref, o_ref, a_ref = refs[pg:]
    cpp = page_size // CMP_STRIDE
    for k in range(pg):
        for wg in range(2 * N_KV):
            w, g = divmod(wg, N_KV)
            r0 = (g * pg + k) * cpp
            for s in range(CMP_STRIDE):
                a_ref[w, r0:r0 + cpp, s * HEAD_DIM:(s + 1) * HEAD_DIM] = page_refs[k][
                    pl.ds(s * ROWS_PER_TOKEN + wg, cpp, stride=CMP_STRIDE * ROWS_PER_TOKEN), :]
    for w in range(2):
        p = jnp.dot(a_ref[w].astype(BF), w1_ref[w], preferred_element_type=F32)
        for g in range(N_KV):
            o_ref[w, g] = p[g * pg * cpp:(g + 1) * pg * cpp]


def _compress_paged(page_table, cache_flat, w1cat, *, page_size):
    bs, n_pages = page_table.shape
    pg = CMP_PAGES_PER_STEP
    assert n_pages % pg == 0
    cpp = page_size // CMP_STRIDE
    page_rows = page_size * ROWS_PER_TOKEN
    kd = CMP_STRIDE * HEAD_DIM
    page_spec = lambda k: pl.BlockSpec((page_rows, HEAD_DIM), lambda b, j, pt: (pt[b, j * pg + k], 0))
    grid_spec = pltpu.PrefetchScalarGridSpec(
        num_scalar_prefetch=1,
        grid=(bs, n_pages // pg),
        in_specs=[page_spec(k) for k in range(pg)] + [pl.BlockSpec((2, kd, 2 * HEAD_DIM), lambda b, j, pt: (0, 0, 0))],
        out_specs=pl.BlockSpec((None, 2, N_KV, pg * cpp, 2 * HEAD_DIM), lambda b, j, pt: (b, 0, 0, j, 0)),
        scratch_shapes=[pltpu.VMEM((2, N_KV * pg * cpp, kd), F32)],
    )
    return pl.pallas_call(
        functools.partial(_compress_paged_kernel, pg=pg, page_size=page_size),
        grid_spec=grid_spec,
        out_shape=jax.ShapeDtypeStruct((bs, 2, N_KV, n_pages * cpp, 2 * HEAD_DIM), F32),
        compiler_params=_cparams(("parallel", "arbitrary")),
        name="compress_paged",
    )(page_table, *([cache_flat] * pg), w1cat)


def _compress_finish(p, pe_term, w2):
    nch = p.shape[0]
    pre = p[:, :HEAD_DIM] + pltpu.roll(p[:, HEAD_DIM:], nch - 1, 0) + pe_term
    return jnp.dot(jax.nn.gelu(pre).astype(BF), w2, preferred_element_type=F32).astype(BF)


def _cmp_attn_paged_kernel(q_ref, pk_ref, pv_ref, pe_ref, w1f_ref, w2_ref, oc_ref, sb_ref, *, tq, nsp, ns, nc, pos0):
    pe_k = jnp.dot(pe_ref[0].astype(BF), w1f_ref[0], preferred_element_type=F32)[0:1, :]
    pe_v = jnp.dot(pe_ref[1].astype(BF), w1f_ref[1], preferred_element_type=F32)[0:1, :]
    kc = _compress_finish(pk_ref[...], pe_k, w2_ref[0])
    vc = _compress_finish(pv_ref[...], pe_v, w2_ref[1])
    _cmp_attn_body(q_ref, kc, vc, oc_ref, sb_ref, pos0, tq=tq, nsp=nsp, ns=ns, nc=nc)


def _cmp_attn_paged(u5, pcmp, pe8, w1flat, w2, *, tq, row0, t_kv, pos0):
    bs, _, _, nch, _ = pcmp.shape
    nc = (t_kv - CMP_BLOCK) // CMP_STRIDE + 1
    ns = -(-t_kv // SEL_BLOCK)
    nsp = -(-ns // LANES) * LANES
    rb0 = row0 // tq
    gw = HPG * HEAD_DIM
    kd = CMP_STRIDE * HEAD_DIM
    kern = functools.partial(_cmp_attn_paged_kernel, tq=tq, nsp=nsp, ns=ns, nc=nc, pos0=pos0)
    return pl.pallas_call(
        kern,
        grid=(bs, N_KV),
        in_specs=[pl.BlockSpec((tq, gw), lambda b, g: (rb0 + b, g)),
                  pl.BlockSpec((None, None, None, nch, 2 * HEAD_DIM), lambda b, g: (b, 0, g, 0, 0)),
                  pl.BlockSpec((None, None, None, nch, 2 * HEAD_DIM), lambda b, g: (b, 1, g, 0, 0)),
                  pl.BlockSpec((2, SUBLANES, 2 * kd), lambda b, g: (0, 0, 0)),
                  pl.BlockSpec((2, 2 * kd, HEAD_DIM), lambda b, g: (0, 0, 0)),
                  pl.BlockSpec((2, HEAD_DIM, HEAD_DIM), lambda b, g: (0, 0, 0))],
        out_specs=[pl.BlockSpec((tq, gw), lambda b, g: (b, g)),
                   pl.BlockSpec((tq, nsp), lambda b, g: (b, g))],
        out_shape=[jax.ShapeDtypeStruct((bs * tq, N_KV * gw), F32),
                   jax.ShapeDtypeStruct((bs * tq, N_KV * nsp), F32)],
        compiler_params=_cparams(("parallel", "arbitrary")),
        name="cmp_attn_paged",
    )(u5, pcmp, pcmp, pe8, w1flat, w2)


def _sel_win_paged_kernel(pt_ref, page_ref, wq_ref, newkv_ref, wcache_ref, newwin_ref, gt_ref, oc_ref, o_ref,
                          m_ref, l_ref, acc_ref, *, n_pages, page_size, nsp, tq, win_keep, past_len):
    j = pl.program_id(1)
    rows = N_HEADS * tq
    kvd = N_KV * HEAD_DIM
    wq = wq_ref[...]
    nt = (((1,), (1,)), ((), ()))

    @pl.when(j == 0)
    def _():
        m_ref[...] = jnp.full(m_ref.shape, -jnp.inf, F32)
        l_ref[...] = jnp.zeros(l_ref.shape, F32)
        acc_ref[...] = jnp.zeros(acc_ref.shape, F32)

    def tile(k_all, v_all, first_block, mask):
        blk = first_block + (lax.broadcasted_iota(jnp.int32, (page_size, nsp), 0) >> SEL_SHIFT)
        onehot = (blk == lax.broadcasted_iota(jnp.int32, (page_size, nsp), 1)).astype(BF)
        s = lax.dot_general(wq, jnp.concatenate([k_all, onehot], axis=1), nt, preferred_element_type=F32) * SCALE
        if mask is not None:
            s = jnp.where(mask, s, NEG)
        m_old = m_ref[...]
        m_new = jnp.maximum(m_old, jnp.max(s, axis=-1, keepdims=True))
        alpha = jnp.exp(m_old - m_new)
        p = jnp.exp(s - m_new)
        l_ref[...] = alpha * l_ref[...] + jnp.sum(p, axis=-1, keepdims=True)
        acc_ref[...] = alpha * acc_ref[...] + jnp.dot(p.astype(BF), v_all, preferred_element_type=F32)
        m_ref[...] = m_new

    def page_rows(which):
        return jnp.concatenate([page_ref[pl.ds(which * N_KV + g, page_size, stride=ROWS_PER_TOKEN), :]
                                for g in range(N_KV)], axis=1).astype(BF)

    @pl.when(j < n_pages)
    def _():
        tile(page_rows(2), page_rows(3), j * (page_size // SEL_BLOCK), None)

    @pl.when(j == n_pages)
    def _():
        qi = lax.broadcasted_iota(jnp.int32, (rows, page_size), 0) & (tq - 1)
        key = lax.broadcasted_iota(jnp.int32, (rows, page_size), 1)
        nk = newkv_ref[...]
        zpad = jnp.zeros((page_size - tq, kvd), F32)
        k_all = jnp.concatenate([nk[:, :kvd], zpad], axis=0).astype(BF)
        v_all = jnp.concatenate([nk[:, kvd:], zpad], axis=0).astype(BF)
        tile(k_all, v_all, n_pages * (page_size // SEL_BLOCK), key <= qi)
        o_s = acc_ref[...] / l_ref[...]

        nw = newwin_ref[...]
        kc = jnp.concatenate([wcache_ref[pl.ds(g, win_keep, stride=2 * N_KV), :] for g in range(N_KV)], axis=1)
        vc = jnp.concatenate([wcache_ref[pl.ds(N_KV + g, win_keep, stride=2 * N_KV), :] for g in range(N_KV)], axis=1)
        kw = jnp.concatenate([kc, nw[:, :kvd], zpad], axis=0).astype(BF)
        vw = jnp.concatenate([vc, nw[:, kvd:], zpad], axis=0).astype(BF)
        nkw = win_keep + page_size
        sw = lax.dot_general(wq[:, :kvd], kw, nt, preferred_element_type=F32) * SCALE
        t = past_len + (lax.broadcasted_iota(jnp.int32, (rows, nkw), 0) & (tq - 1))
        kpos = (past_len - win_keep) + lax.broadcasted_iota(jnp.int32, (rows, nkw), 1)
        sw = jnp.where((kpos <= t) & (kpos > t - WINDOW), sw, NEG)
        pw = jnp.exp(sw - jnp.max(sw, axis=-1, keepdims=True))
        o_w = jnp.dot(pw.astype(BF), vw, preferred_element_type=F32) / jnp.sum(pw, axis=-1, keepdims=True)

        gts = gt_ref[...]
        oc = oc_ref[...]
        lane = lax.broadcasted_iota(jnp.int32, gts.shape, 1)
        for hh in range(N_HEADS):
            g = hh // HPG
            gc = jnp.sum(jnp.where(lane == hh * 3, gts, 0.0), axis=-1, keepdims=True)
            gs = jnp.sum(jnp.where(lane == hh * 3 + 1, gts, 0.0), axis=-1, keepdims=True)
            gw = jnp.sum(jnp.where(lane == hh * 3 + 2, gts, 0.0), axis=-1, keepdims=True)
            rs = slice(hh * tq, (hh + 1) * tq)
            cs = slice(g * HEAD_DIM, (g + 1) * HEAD_DIM)
            os_ = slice(hh * HEAD_DIM, (hh + 1) * HEAD_DIM)
            o_ref[:, os_] = gc * oc[:, os_] + gs * o_s[rs, cs] + gw * o_w[rs, cs]


def _sel_win_paged(page_table, cache_flat, wq, u5, wcache_flat, gates, oc, *, tq, row0, page_size, win_keep, past_len):
    bs, n_pages = page_table.shape
    rows = N_HEADS * tq
    kvd = N_KV * HEAD_DIM
    nsp = wq.shape[2] - kvd
    page_rows = page_size * ROWS_PER_TOKEN
    rb0 = row0 // tq
    sel_col = (N_HEADS * HEAD_DIM + 2 * kvd) // (2 * kvd)
    kern = functools.partial(_sel_win_paged_kernel, n_pages=n_pages, page_size=page_size, nsp=nsp, tq=tq,
                             win_keep=win_keep, past_len=past_len)
    grid_spec = pltpu.PrefetchScalarGridSpec(
        num_scalar_prefetch=1,
        grid=(bs, n_pages + 1),
        in_specs=[pl.BlockSpec((page_rows, HEAD_DIM), lambda b, j, pt: (pt[b, jnp.minimum(j, n_pages - 1)], 0)),
                  pl.BlockSpec((None, rows, kvd + nsp), lambda b, j, pt: (b, 0, 0)),
                  pl.BlockSpec((tq, 2 * kvd), lambda b, j, pt: (rb0 + b, sel_col)),
                  pl.BlockSpec((win_keep * 2 * N_KV, HEAD_DIM), lambda b, j, pt: (b, 0)),
                  pl.BlockSpec((tq, 2 * kvd), lambda b, j, pt: (rb0 + b, sel_col + 1)),
                  pl.BlockSpec((tq, LANES), lambda b, j, pt: (rb0 + b, 0)),
                  pl.BlockSpec((tq, N_HEADS * HEAD_DIM), lambda b, j, pt: (b, 0))],
        out_specs=pl.BlockSpec((tq, N_HEADS * HEAD_DIM), lambda b, j, pt: (b, 0)),
        scratch_shapes=[pltpu.VMEM((rows, 1), F32), pltpu.VMEM((rows, 1), F32), pltpu.VMEM((rows, kvd), F32)],
    )
    return pl.pallas_call(
        kern,
        grid_spec=grid_spec,
        out_shape=jax.ShapeDtypeStruct((bs * tq, N_HEADS * HEAD_DIM), F32),
        compiler_params=_cparams(("parallel", "arbitrary")),
        name="sel_win_paged",
    )(page_table, cache_flat, wq, u5, wcache_flat, u5, gates, oc)


def _gelu(u):
    return jax.nn.gelu(u)


def _identity(u):
    return u


def kernel(x_prompt, x_sample, state_conv, state_h, cache_kv, cache_win, page_table, p_prompt, p_sample,
           lru_w_in, lru_conv_w, lru_conv_b, lru_w_a, lru_b_a, lru_w_x, lru_b_x, lru_lambda, lru_w_out,
           nsa_w_in, nsa_cmp_w1, nsa_cmp_pe, nsa_cmp_w2, nsa_w_out,
           ln1_g, ln1_b, ln2_g, ln2_b,
           moe_w_group, moe_b_group, moe_w_expert, moe_b_expert, moe_w_gate, moe_w_up, moe_w_down,
           ple_w_gate, ple_w_proj):
    bp, tp, d = x_prompt.shape
    bs, ts, _ = x_sample.shape
    assert bp == 1 and ts == SUBLANES
    n_p = bp * tp
    n_s = bs * ts
    n = n_p + n_s
    past_len = page_table.shape[1] * cache_kv.shape[2]
    d_rnn = lru_w_out.shape[1]
    hd = N_HEADS * HEAD_DIM
    kvw = 2 * N_KV * HEAD_DIM

    x0 = jnp.concatenate([x_prompt.reshape(n_p, d), x_sample.reshape(n_s, d)], axis=0)

    def moe_and_ple(x1, layer):
        w_route = jnp.zeros((d, LANES), F32).at[:, :N_EXPERTS].set(moe_w_expert[layer])
        w_route = w_route.at[:, N_EXPERTS:N_EXPERTS + N_GROUPS].set(moe_w_group[layer])
        b_route = jnp.zeros((1, LANES), F32).at[0, :N_EXPERTS].set(moe_b_expert[layer])
        b_route = b_route.at[0, N_EXPERTS:N_EXPERTS + N_GROUPS].set(moe_b_group[layer])
        y0, y1 = _moe_layer(x1, w_route, b_route, moe_w_gate, moe_w_up, moe_w_down, layer)
        p_cat = jnp.concatenate([p_prompt[layer].reshape(n_p, -1), p_sample[layer].reshape(n_s, -1)], axis=0)
        return _ln_ple(x1, y0, y1, ln2_g[layer].reshape(1, d), ln2_b[layer].reshape(1, d),
                       ple_w_gate[layer].astype(BF), p_cat, ple_w_proj[layer].astype(BF))

    w_in = lru_w_in[0].astype(BF)
    x0b = x0.astype(BF)
    gate = _mm_act(x0b, w_in[:, :d_rnn], _gelu)
    xr = _mm_act(x0b, w_in[:, d_rnn:], _identity)
    lru_args = (lru_conv_w[0], lru_conv_b[0].reshape(1, d_rnn), lru_w_a[0].astype(BF), lru_b_a[0].reshape(1, d_rnn),
                lru_w_x[0].astype(BF), lru_b_x[0].reshape(1, d_rnn), lru_lambda[0].reshape(1, d_rnn))
    hg_p, hl_p = _rglru_core(xr[:n_p], gate[:n_p], jnp.zeros((1, SUBLANES, d_rnn), F32),
                             jnp.zeros((1, 1, d_rnn), F32), *lru_args, chain=True, tt=_row_tile(n_p, 512))
    prev_s = jnp.pad(state_conv[0], ((0, 0), (SUBLANES - (CONV_W - 1), 0), (0, 0)))
    hg_s, hl_s = _rglru_core(xr[n_p:], gate[n_p:], prev_s, state_h[0].reshape(bs, 1, d_rnn),
                             *lru_args, chain=False, tt=n_s)
    hg = jnp.concatenate([hg_p, hg_s], axis=0)
    conv_prompt = xr[n_p - (CONV_W - 1):n_p].reshape(1, bp, CONV_W - 1, d_rnn)
    h_prompt = hl_p[n_p // SUBLANES - 1].reshape(1, bp, d_rnn)
    conv_sample = xr[n_p:].reshape(bs, ts, d_rnn)[:, ts - (CONV_W - 1):].reshape(1, bs, CONV_W - 1, d_rnn)
    h_sample = hl_s.reshape(1, bs, d_rnn)
    x1 = _mm_res_ln(hg, lru_w_out[0].astype(BF), x0, ln1_g[0].reshape(1, d), ln1_b[0].reshape(1, d))
    xa = moe_and_ple(x1, 0)

    w_nsa = nsa_w_in[0]
    n_main = hd + 3 * kvw
    pos = jnp.concatenate([jnp.arange(n_p, dtype=jnp.int32) % tp,
                           past_len + jnp.arange(n_s, dtype=jnp.int32) % ts])
    half = HEAD_DIM // 2
    inv = ROPE_THETA ** (-jnp.arange(half, dtype=F32) / half)
    ang = pos.astype(F32)[:, None] * inv[None, :]
    cos = jnp.concatenate([jnp.cos(ang), jnp.cos(ang)], axis=1)
    sin_signed = jnp.concatenate([-jnp.sin(ang), jnp.sin(ang)], axis=1)
    xab = xa.astype(BF)
    u5 = _nsa_in(xab, w_nsa[:, :n_main].astype(BF), cos, sin_signed)
    w_gates = jnp.zeros((d, LANES), F32).at[:, :3 * N_HEADS].set(w_nsa[:, n_main:]).astype(BF)
    gates = _mm_act(xab, w_gates, jax.nn.sigmoid)

    kv_prompt = u5[:n_p, hd:hd + 2 * kvw].reshape(1, bp, tp, 4, N_KV, HEAD_DIM)
    kv_sample = u5[n_p:, hd:hd + 2 * kvw].reshape(1, bs, ts, 4, N_KV, HEAD_DIM)
    win_new_p = u5[:n_p, hd + 2 * kvw:]
    win_new_s = u5[n_p:, hd + 2 * kvw:].reshape(bs, ts, kvw)
    win_keep_p = min(WINDOW, tp)
    win_prompt = win_new_p[n_p - win_keep_p:].reshape(1, bp, win_keep_p, 2, N_KV, HEAD_DIM)
    win_all_s = jnp.concatenate([cache_win[0].reshape(bs, -1, kvw), win_new_s], axis=1)
    win_keep_s = cache_win.shape[2]
    win_sample = win_all_s[:, -win_keep_s:].reshape(1, bs, win_keep_s, 2, N_KV, HEAD_DIM)

    kd = CMP_STRIDE * HEAD_DIM
    w1 = nsa_cmp_w1[0]
    w1flat = w1.reshape(2, 2 * kd, HEAD_DIM).astype(BF)
    w1cat = jnp.concatenate([w1flat[:, :kd], w1flat[:, kd:]], axis=2)
    pe8 = jnp.zeros((2, SUBLANES, 2 * kd), F32).at[:, 0].set(nsa_cmp_pe[0].reshape(2, 2 * kd))
    w2 = nsa_cmp_w2[0].astype(BF)

    kcvc_p = _compress(u5[:n_p, hd:hd + kvw].reshape(bp, tp, kvw), w1cat, pe8, w1flat, w2)
    tq_p = _row_tile(tp, 128)
    tk_p = _row_tile(tp, 512)
    oc_p, sb_p = _cmp_attn(u5, kcvc_p, nseq=bp, tq_total=tp, tq=tq_p, row0=0, t_kv=tp, pos0=0)
    ksvs_p = u5[:n_p, hd + kvw:hd + 2 * kvw].astype(BF)
    kwvw_p = win_new_p.astype(BF)
    o_p = _sel_win_attn(u5, sb_p, gates, oc_p, ksvs_p, kwvw_p, tq=tq_p, tk=tk_p)

    page_size = cache_kv.shape[2]
    cache_flat = cache_kv[0].reshape(-1, HEAD_DIM)
    pcmp = _compress_paged(page_table, cache_flat, w1cat, page_size=page_size)
    t_kv_s = past_len + ts
    oc_s, sb_s = _cmp_attn_paged(u5, pcmp, pe8, w1flat, w2, tq=ts, row0=n_p, t_kv=t_kv_s, pos0=past_len)
    nsp_s = sb_s.shape[1] // N_KV
    q5 = u5[n_p:, :hd].reshape(bs, ts, N_KV, HPG, HEAD_DIM)
    eye = jnp.eye(N_KV, dtype=F32)[None, None, :, None, :, None]
    q_diag = (q5[:, :, :, :, None, :] * eye).transpose(0, 2, 3, 1, 4, 5).reshape(bs, N_HEADS * ts, N_KV * HEAD_DIM)
    bias = jnp.broadcast_to(sb_s.reshape(bs, ts, N_KV, 1, nsp_s).transpose(0, 2, 3, 1, 4),
                            (bs, N_KV, HPG, ts, nsp_s)).reshape(bs, N_HEADS * ts, nsp_s)
    wq = jnp.concatenate([q_diag, bias], axis=2).astype(BF)
    o_s = _sel_win_paged(page_table, cache_flat, wq, u5, cache_win[0].reshape(-1, HEAD_DIM), gates, oc_s,
                         tq=ts, row0=n_p, page_size=page_size, win_keep=win_keep_s, past_len=past_len)

    o_all = jnp.concatenate([o_p, o_s], axis=0)
    x1 = _mm_res_ln(o_all, nsa_w_out[0].astype(BF), xa, ln1_g[1].reshape(1, d), ln1_b[1].reshape(1, d))
    xo = moe_and_ple(x1, 1)

    y_prompt = xo[:n_p].reshape(bp, tp, d)
    y_sample = xo[n_p:].reshape(bs, ts, d)
    return (y_prompt, y_sample, conv_prompt, h_prompt, kv_prompt, win_prompt,
            conv_sample, h_sample, kv_sample, win_sample)
```

```python
import functools

import jax
import jax.numpy as jnp
from jax import lax
from jax.experimental import pallas as pl
from jax.experimental.pallas import tpu as pltpu

F32 = jnp.float32
BF = jnp.bfloat16

HEAD_DIM = 128
N_HEADS = 16
N_KV = 4
HPG = N_HEADS // N_KV
LRU_BLOCK = 256
CONV_W = 4
LRU_C = 8.0
CMP_BLOCK = 32
CMP_STRIDE = 16
SEL_BLOCK = 64
N_SEL = 16
WINDOW = 512
ROPE_THETA = 10000.0
N_GROUPS = 4
EPG = 8
N_EXPERTS = N_GROUPS * EPG
DEPTH = 2
ALPHA = (2 * DEPTH) ** 0.25
LN_EPS = 1e-5
NEG = -1e30
FORCE_BONUS = 1e4
SCALE = HEAD_DIM ** -0.5
SEL_SHIFT = SEL_BLOCK.bit_length() - 1
EPG_SHIFT = EPG.bit_length() - 1

LANES = 128
SUBLANES = 8
VMEM_LIMIT = 56 * 1024 * 1024
MOE_TM = 1024
MOE_SUB = 256
MOE_TF = 256


def _cparams(sem):
    return pltpu.CompilerParams(dimension_semantics=sem, vmem_limit_bytes=VMEM_LIMIT)


def _row_tile(n, cap):
    best = None
    for t in range(SUBLANES, cap + 1, SUBLANES):
        if n % t == 0:
            best = t
    assert best is not None, (n, cap)
    return best


def _mm_act_kernel(x_ref, w_ref, o_ref, *, act):
    u = jnp.dot(x_ref[...].astype(BF), w_ref[...], preferred_element_type=F32)
    o_ref[...] = act(u).astype(o_ref.dtype)


def _mm_act(x, w, act, out_dtype=F32, tn=512):
    m, k = x.shape
    n = w.shape[1]
    tm = _row_tile(m, 768)
    tn = min(tn, n)
    assert n % tn == 0
    return pl.pallas_call(
        functools.partial(_mm_act_kernel, act=act),
        grid=(m // tm, n // tn),
        in_specs=[pl.BlockSpec((tm, k), lambda i, j: (i, 0)),
                  pl.BlockSpec((k, tn), lambda i, j: (0, j))],
        out_specs=pl.BlockSpec((tm, tn), lambda i, j: (i, j)),
        out_shape=jax.ShapeDtypeStruct((m, n), out_dtype),
        compiler_params=_cparams(("parallel", "arbitrary")),
        name="mm_act",
    )(x, w)


def _rglru_kernel(xr_ref, prev_ref, h0_ref, gate_ref, cw_ref, cb_ref, wa_ref, ba_ref, wx_ref, bx_ref,
                  lam_ref, hg_ref, hlast_ref, carry_prev, carry_h, h_scr, *, chain):
    tt, c = xr_ref.shape
    g = tt // SUBLANES
    x = xr_ref[...].reshape(g, SUBLANES, c)
    if chain:
        first = pl.program_id(1) == 0
        p0 = jnp.where(first, prev_ref[...], carry_prev[...])
        prev = jnp.concatenate([p0, x[:-1]], axis=0) if g > 1 else p0
        h_start = jnp.where(first, h0_ref[...], carry_h[...])
    else:
        prev = prev_ref[...]
    sub = lax.broadcasted_iota(jnp.int32, (g, SUBLANES, c), 1)
    cw = [cw_ref[k:k + 1, :].reshape(1, 1, c) for k in range(CONV_W)]
    xc = cb_ref[...].reshape(1, 1, c) + x * cw[CONV_W - 1]
    for s in range(1, CONV_W):
        sh = jnp.where(sub < s, pltpu.roll(prev, s, 1), pltpu.roll(x, s, 1))
        xc = xc + sh * cw[CONV_W - 1 - s]
    xc2 = xc.reshape(tt, c)
    xb = xc2.astype(BF)
    r = jax.nn.sigmoid(jnp.dot(xb, wa_ref[...], preferred_element_type=F32) + ba_ref[...])
    i = jax.nn.sigmoid(jnp.dot(xb, wx_ref[...], preferred_element_type=F32) + bx_ref[...])
    log_a = (-LRU_C * r) * jax.nn.softplus(-lam_ref[...])
    a = jnp.exp(log_a)
    bterm = jnp.sqrt(-jnp.tanh(log_a) * (a * a + 1.0)) * (i * xc2)
    a3 = a.reshape(g, SUBLANES, c)
    b3 = bterm.reshape(g, SUBLANES, c)
    for d in (1, 2, 4):
        a_sh = jnp.where(sub < d, 1.0, pltpu.roll(a3, d, 1))
        b_sh = jnp.where(sub < d, 0.0, pltpu.roll(b3, d, 1))
        b3 = a3 * b_sh + b3
        a3 = a3 * a_sh
    if chain:
        hs = []
        hprev = h_start.reshape(1, c)
        for j in range(g):
            hj = a3[j] * hprev + b3[j]
            hs.append(hj)
            hprev = hj[SUBLANES - 1:SUBLANES, :]
        h3 = jnp.stack(hs, axis=0)
        carry_prev[...] = x[g - 1:g]
        carry_h[...] = hprev.reshape(1, 1, c)
    else:
        h3 = a3 * h0_ref[...] + b3
    h2 = h3.reshape(tt, c)
    hg_ref[...] = (h2 * gate_ref[...]).astype(hg_ref.dtype)
    for k in range(c // LANES):
        h_scr[k] = h2[:, k * LANES:(k + 1) * LANES]
        hlast_ref[:, k * LANES:(k + 1) * LANES] = h_scr[k, pl.ds(SUBLANES - 1, g, stride=SUBLANES), :]


def _rglru_core(xr, gate, prev, h0, cw, cb, wa, ba, wx, bx, lam, *, chain, tt, row0, n):
    d = xr.shape[1]
    rb0 = row0 // tt
    assert row0 % tt == 0 and n % tt == 0
    c = LRU_BLOCK
    g = tt // SUBLANES
    pg = 1 if chain else g
    vec = lambda: pl.BlockSpec((1, c), lambda nb, t: (0, nb))
    return pl.pallas_call(
        functools.partial(_rglru_kernel, chain=chain),
        grid=(d // c, n // tt),
        in_specs=[pl.BlockSpec((tt, c), lambda nb, t: (rb0 + t, nb)),
                  pl.BlockSpec((pg, SUBLANES, c), lambda nb, t: (0 if chain else t, 0, nb)),
                  pl.BlockSpec((pg, 1, c), lambda nb, t: (0 if chain else t, 0, nb)),
                  pl.BlockSpec((tt, c), lambda nb, t: (rb0 + t, nb)),
                  pl.BlockSpec((CONV_W, c), lambda nb, t: (0, nb)),
                  vec(),
                  pl.BlockSpec((None, c, c), lambda nb, t: (nb, 0, 0)),
                  vec(),
                  pl.BlockSpec((None, c, c), lambda nb, t: (nb, 0, 0)),
                  vec(), vec()],
        out_specs=[pl.BlockSpec((tt, c), lambda nb, t: (t, nb)),
                   pl.BlockSpec((g, c), lambda nb, t: (t, nb))],
        out_shape=[jax.ShapeDtypeStruct((n, d), BF), jax.ShapeDtypeStruct((n // SUBLANES, d), F32)],
        scratch_shapes=[pltpu.VMEM((1, SUBLANES, c), F32), pltpu.VMEM((1, 1, c), F32),
                        pltpu.VMEM((c // LANES, tt, LANES), F32)],
        compiler_params=_cparams(("parallel", "arbitrary")),
        name="rglru_core",
    )(xr, prev, h0, gate, cw, cb, wa, ba, wx, bx, lam)


def _layer_norm(z, g, b):
    mu = jnp.mean(z, axis=-1, keepdims=True)
    zc = z - mu
    var = jnp.mean(zc * zc, axis=-1, keepdims=True)
    return zc * lax.rsqrt(var + LN_EPS) * g + b


def _mm_res_ln_kernel(a_ref, w_ref, res_ref, g_ref, b_ref, o_ref):
    y = jnp.dot(a_ref[...].astype(BF), w_ref[...], preferred_element_type=F32)
    o_ref[...] = _layer_norm(ALPHA * res_ref[...] + y, g_ref[...], b_ref[...])


def _mm_res_ln(a, w, res, g, b):
    m, k = a.shape
    d = w.shape[1]
    tm = _row_tile(m, 256)
    return pl.pallas_call(
        _mm_res_ln_kernel,
        grid=(m // tm,),
        in_specs=[pl.BlockSpec((tm, k), lambda i: (i, 0)),
                  pl.BlockSpec((k, d), lambda i: (0, 0)),
                  pl.BlockSpec((tm, d), lambda i: (i, 0)),
                  pl.BlockSpec((1, d), lambda i: (0, 0)),
                  pl.BlockSpec((1, d), lambda i: (0, 0))],
        out_specs=pl.BlockSpec((tm, d), lambda i: (i, 0)),
        out_shape=jax.ShapeDtypeStruct((m, d), F32),
        compiler_params=_cparams(("parallel",)),
        name="mm_res_ln",
    )(a, w, res, g, b)


def _route_kernel(x_ref, w_ref, b_ref, o_ref):
    logits = jnp.dot(x_ref[...], w_ref[...], preferred_element_type=F32,
                     precision=lax.Precision.HIGHEST) + b_ref[...]
    tm = logits.shape[0]
    lane = lax.broadcasted_iota(jnp.int32, (tm, LANES), 1)
    lanef = lane.astype(F32)
    big = float(LANES)
    ninf = -jnp.inf
    is_grp = (lane >= N_EXPERTS) & (lane < N_EXPERTS + N_GROUPS)
    lg = jnp.where(is_grp, logits, ninf)
    mg = jnp.max(lg, axis=-1, keepdims=True)
    p_top = 1.0 / jnp.sum(jnp.where(is_grp, jnp.exp(lg - mg), 0.0), axis=-1, keepdims=True)
    g_sel = jnp.min(jnp.where(lg == mg, lanef, big), axis=-1, keepdims=True) - float(N_EXPERTS)
    in_grp = (lane < N_EXPERTS) & ((lane >> EPG_SHIFT).astype(F32) == g_sel)
    le = jnp.where(in_grp, logits, ninf)
    v1 = jnp.max(le, axis=-1, keepdims=True)
    j1 = jnp.min(jnp.where(le == v1, lanef, big), axis=-1, keepdims=True)
    le2 = jnp.where(lanef == j1, ninf, le)
    v2 = jnp.max(le2, axis=-1, keepdims=True)
    j2 = jnp.min(jnp.where(le2 == v2, lanef, big), axis=-1, keepdims=True)
    e2 = jnp.exp(v2 - v1)
    den = 1.0 + e2
    w1 = (1.0 / den) * p_top
    w2 = (e2 / den) * p_top
    out = jnp.where(lane == 0, j1, jnp.where(lane == 1, j2, jnp.where(lane == 2, w1, jnp.where(lane == 3, w2, 0.0))))
    o_ref[...] = out


def _route(x, w_cat, b_cat):
    m, d = x.shape
    tm = _row_tile(m, 256)
    return pl.pallas_call(
        _route_kernel,
        grid=(m // tm,),
        in_specs=[pl.BlockSpec((tm, d), lambda i: (i, 0)),
                  pl.BlockSpec((d, LANES), lambda i: (0, 0)),
                  pl.BlockSpec((1, LANES), lambda i: (0, 0))],
        out_specs=pl.BlockSpec((tm, LANES), lambda i: (i, 0)),
        out_shape=jax.ShapeDtypeStruct((m, LANES), F32),
        compiler_params=_cparams(("parallel",)),
        name="moe_route",
    )(x, w_cat, b_cat)


def _moe_kernel(te_ref, ns_ref, st_ref, x_ref, wt_ref, wg_ref, wu_ref, wd_ref, o_ref):
    i = pl.program_id(0)
    f = pl.program_id(1)

    @pl.when(f == 0)
    def _():
        o_ref[...] = jnp.zeros_like(o_ref)

    @pl.when(ns_ref[i] > 0)
    def _():
        wg = wg_ref[...].astype(BF)
        wu = wu_ref[...].astype(BF)
        wd = wd_ref[...].astype(BF)

        def body(s, carry):
            r0 = pl.multiple_of(s * MOE_SUB, MOE_SUB)
            xs = x_ref[pl.ds(r0, MOE_SUB), :].astype(BF)
            hg = jnp.dot(xs, wg, preferred_element_type=F32)
            hu = jnp.dot(xs, wu, preferred_element_type=F32)
            h = (jax.nn.silu(hg) * hu) * wt_ref[pl.ds(r0, MOE_SUB), :]
            o_ref[pl.ds(r0, MOE_SUB), :] += jnp.dot(h.astype(BF), wd, preferred_element_type=F32)
            return carry

        lax.fori_loop(0, ns_ref[i], body, 0)


def _moe_ffn(tile_expert, tile_nsub, tile_start, xg, wt, w_gate, w_up, w_down, layer):
    _, d = xg.shape
    nt = tile_expert.shape[0]
    ff = w_gate.shape[3]
    grid_spec = pltpu.PrefetchScalarGridSpec(
        num_scalar_prefetch=3,
        grid=(nt, ff // MOE_TF),
        in_specs=[pl.BlockSpec((pl.Element(MOE_TM), pl.Element(d)), lambda i, f, te, ns, st: (st[i] * MOE_SUB, 0)),
                  pl.BlockSpec((pl.Element(MOE_TM), pl.Element(1)), lambda i, f, te, ns, st: (st[i] * MOE_SUB, 0)),
                  pl.BlockSpec((None, None, d, MOE_TF), lambda i, f, te, ns, st: (layer, te[i], 0, f)),
                  pl.BlockSpec((None, None, d, MOE_TF), lambda i, f, te, ns, st: (layer, te[i], 0, f)),
                  pl.BlockSpec((None, None, MOE_TF, d), lambda i, f, te, ns, st: (layer, te[i], f, 0))],
        out_specs=pl.BlockSpec((MOE_TM, d), lambda i, f, te, ns, st: (i, 0)),
    )
    return pl.pallas_call(
        _moe_kernel,
        grid_spec=grid_spec,
        out_shape=jax.ShapeDtypeStruct((nt * MOE_TM, d), F32),
        compiler_params=_cparams(("arbitrary", "arbitrary")),
        name="moe_ffn",
    )(tile_expert, tile_nsub, tile_start, xg, wt, w_gate, w_up, w_down)


def _ln_ple_kernel(x1_ref, y0_ref, y1_ref, g_ref, b_ref, wg_ref, p_ref, wp_ref, o_ref):
    z = ALPHA * x1_ref[...] + (y0_ref[...] + y1_ref[...])
    x2 = _layer_norm(z, g_ref[...], b_ref[...])
    gate = jax.nn.sigmoid(jnp.dot(x2.astype(BF), wg_ref[...], preferred_element_type=F32))
    proj = jnp.dot(p_ref[...].astype(BF), wp_ref[...], preferred_element_type=F32)
    o_ref[...] = x2 + gate * proj


def _ln_ple(x1, y0, y1, g, b, w_gate, p, w_proj):
    m, d = x1.shape
    pd = p.shape[1]
    tm = _row_tile(m, 256)
    row = lambda w: pl.BlockSpec((tm, w), lambda i: (i, 0))
    return pl.pallas_call(
        _ln_ple_kernel,
        grid=(m // tm,),
        in_specs=[row(d), row(d), row(d),
                  pl.BlockSpec((1, d), lambda i: (0, 0)),
                  pl.BlockSpec((1, d), lambda i: (0, 0)),
                  pl.BlockSpec((d, d), lambda i: (0, 0)),
                  row(pd),
                  pl.BlockSpec((pd, d), lambda i: (0, 0))],
        out_specs=row(d),
        out_shape=jax.ShapeDtypeStruct((m, d), F32),
        compiler_params=_cparams(("parallel",)),
        name="ln_ple",
    )(x1, y0, y1, g, b, w_gate, p, w_proj)


def _moe_layer(x1, w_route, b_route, w_gate, w_up, w_down, layer):
    n, d = x1.shape
    routed = _route(x1, w_route, b_route)
    e_tok = routed[:, 0:2].astype(jnp.int32)
    w_tok = routed[:, 2:4]
    a = 2 * n
    sub_per_tile = MOE_TM // MOE_SUB
    nt = a // MOE_TM + N_EXPERTS
    n_sub_max = a // MOE_SUB + N_EXPERTS
    e_flat = e_tok.reshape(a)
    onehot = (e_flat[:, None] == jnp.arange(N_EXPERTS, dtype=jnp.int32)[None, :]).astype(jnp.int32)
    csum = jnp.cumsum(onehot, axis=0)
    counts = csum[a - 1]
    rank = jnp.sum(csum * onehot, axis=1) - 1
    sub_per = (counts + MOE_SUB - 1) // MOE_SUB
    sub_off = jnp.cumsum(sub_per) - sub_per
    tiles_per = (counts + MOE_TM - 1) // MOE_TM
    tile_end = jnp.cumsum(tiles_per)
    tile_off = tile_end - tiles_per
    src_row = sub_off[e_flat] * MOE_SUB + rank
    dst_row = (tile_off[e_flat] + rank // MOE_TM) * MOE_TM + rank % MOE_TM
    rows_in = (n_sub_max + sub_per_tile) * MOE_SUB
    tok_pad = jnp.zeros((rows_in,), jnp.int32).at[src_row].set(jnp.arange(a, dtype=jnp.int32) // 2)
    w_pad = jnp.zeros((rows_in,), F32).at[src_row].set(w_tok.reshape(a))
    tile_id = jnp.arange(nt, dtype=jnp.int32)
    n_used = tile_end[N_EXPERTS - 1]
    tile_expert = jnp.minimum(jnp.sum((tile_end[None, :] <= tile_id[:, None]).astype(jnp.int32), axis=1),
                              N_EXPERTS - 1)
    k_in_expert = tile_id - tile_off[tile_expert]
    used = tile_id < n_used
    tile_nsub = jnp.where(used, jnp.clip(sub_per[tile_expert] - k_in_expert * sub_per_tile, 0, sub_per_tile), 0)
    tile_start = jnp.where(used, sub_off[tile_expert] + k_in_expert * sub_per_tile, 0)
    last_used = jnp.sum(jnp.where(tile_id == n_used - 1, tile_expert, 0))
    tile_expert = jnp.where(used, tile_expert, last_used)
    xg = x1.at[tok_pad].get(mode="promise_in_bounds")
    yg = _moe_ffn(tile_expert.astype(jnp.int32), tile_nsub.astype(jnp.int32), tile_start.astype(jnp.int32),
                  xg, w_pad.reshape(-1, 1), w_gate, w_up, w_down, layer)
    dst2 = dst_row.reshape(n, 2)
    return (yg.at[dst2[:, 0]].get(mode="promise_in_bounds"), yg.at[dst2[:, 1]].get(mode="promise_in_bounds"))


def _nsa_in_kernel(x_ref, w_ref, cos_ref, sin_ref, o_ref, ob_ref, *, n_q_blocks):
    j = pl.program_id(1)
    u = jnp.dot(x_ref[...].astype(BF), w_ref[...], preferred_element_type=F32)
    cos = cos_ref[...]
    sin = sin_ref[...]
    heads = u.shape[1] // HEAD_DIM
    for h in range(heads):
        seg = u[:, h * HEAD_DIM:(h + 1) * HEAD_DIM]
        rot = seg * cos + pltpu.roll(seg, HEAD_DIM // 2, 1) * sin
        if h < heads // 2:
            val = rot
        else:
            val = jnp.where(j < n_q_blocks, rot, seg)
        o_ref[:, h * HEAD_DIM:(h + 1) * HEAD_DIM] = val

    @pl.when(j > n_q_blocks)
    def _():
        ob_ref[...] = o_ref[...].astype(BF)


def _nsa_in(xb, w, cos, sin_signed):
    m, k = xb.shape
    n = w.shape[1]
    tn = 2 * N_KV * HEAD_DIM
    tm = _row_tile(m, 768)
    nqb = (N_HEADS * HEAD_DIM) // tn
    return pl.pallas_call(
        functools.partial(_nsa_in_kernel, n_q_blocks=nqb),
        grid=(m // tm, n // tn),
        in_specs=[pl.BlockSpec((tm, k), lambda i, j: (i, 0)),
                  pl.BlockSpec((k, tn), lambda i, j: (0, j)),
                  pl.BlockSpec((tm, HEAD_DIM), lambda i, j: (i, 0)),
                  pl.BlockSpec((tm, HEAD_DIM), lambda i, j: (i, 0))],
        out_specs=[pl.BlockSpec((tm, tn), lambda i, j: (i, j)),
                   pl.BlockSpec((tm, tn), lambda i, j: (i, jnp.maximum(j - (nqb + 1), 0)))],
        out_shape=[jax.ShapeDtypeStruct((m, n), F32), jax.ShapeDtypeStruct((m, n - (nqb + 1) * tn), BF)],
        compiler_params=_cparams(("parallel", "arbitrary")),
        name="nsa_in",
    )(xb, w, cos, sin_signed)


def _compress_kernel(rows_ref, w1_ref, pe_ref, w1f_ref, w2_ref, o_ref, a_ref):
    t = rows_ref.shape[0]
    nch = t // CMP_STRIDE
    for s in range(CMP_STRIDE):
        a_ref[:, s * HEAD_DIM:(s + 1) * HEAD_DIM] = rows_ref[pl.ds(s, nch, stride=CMP_STRIDE), :].astype(BF)
    p = jnp.dot(a_ref[...], w1_ref[...], preferred_element_type=F32)
    pe_term = jnp.dot(pe_ref[...].astype(BF), w1f_ref[...], preferred_element_type=F32)[0:1, :]
    pre = p[:, :HEAD_DIM] + pltpu.roll(p[:, HEAD_DIM:], nch - 1, 0) + pe_term
    o_ref[...] = jnp.dot(jax.nn.gelu(pre).astype(BF), w2_ref[...], preferred_element_type=F32).astype(o_ref.dtype)


def _compress(rows, w1cat, pe8, w1flat, w2):
    nseq, t, _ = rows.shape
    nch = t // CMP_STRIDE
    kd = CMP_STRIDE * HEAD_DIM
    return pl.pallas_call(
        _compress_kernel,
        grid=(nseq, 2, N_KV),
        in_specs=[pl.BlockSpec((None, t, HEAD_DIM), lambda b, w, g: (b, 0, w * N_KV + g)),
                  pl.BlockSpec((None, kd, 2 * HEAD_DIM), lambda b, w, g: (w, 0, 0)),
                  pl.BlockSpec((None, SUBLANES, 2 * kd), lambda b, w, g: (w, 0, 0)),
                  pl.BlockSpec((None, 2 * kd, HEAD_DIM), lambda b, w, g: (w, 0, 0)),
                  pl.BlockSpec((None, HEAD_DIM, HEAD_DIM), lambda b, w, g: (w, 0, 0))],
        out_specs=pl.BlockSpec((None, None, None, nch, HEAD_DIM), lambda b, w, g: (b, w, g, 0, 0)),
        out_shape=jax.ShapeDtypeStruct((nseq, 2, N_KV, nch, HEAD_DIM), BF),
        scratch_shapes=[pltpu.VMEM((nch, kd), BF)],
        compiler_params=_cparams(("parallel", "arbitrary", "arbitrary")),
        name="compress",
    )(rows, w1cat, pe8, w1flat, w2)


def _stack_heads(qblk, tq):
    return jnp.concatenate([qblk[:, h * HEAD_DIM:(h + 1) * HEAD_DIM] for h in range(HPG)], axis=0)


def _cmp_attn_kernel(q_ref, kc_ref, vc_ref, oc_ref, sb_ref, *, tq, nsp, ns, nc, pos0):
    q0 = pos0 + pl.program_id(1) * tq
    _cmp_attn_body(q_ref, [kc_ref[g] for g in range(N_KV)], [vc_ref[g] for g in range(N_KV)], oc_ref, sb_ref, q0,
                   tq=tq, nsp=nsp, ns=ns, nc=nc)


def _cmp_attn_body(q_ref, kcs, vcs, oc_ref, sb_ref, q0, *, tq, nsp, ns, nc):
    ncp = kcs[0].shape[0]
    rows = HPG * tq
    gw = HPG * HEAD_DIM
    nt = (((1,), (1,)), ((), ()))
    qs = [_stack_heads(q_ref[:, g * gw:(g + 1) * gw], tq).astype(BF) for g in range(N_KV)]
    ss = [lax.dot_general(qs[g], kcs[g], nt, preferred_element_type=F32) * SCALE for g in range(N_KV)]
    t = q0 + (lax.broadcasted_iota(jnp.int32, (rows, ncp), 0) & (tq - 1))
    n = lax.broadcasted_iota(jnp.int32, (rows, ncp), 1)
    mask = (n * CMP_STRIDE + (CMP_BLOCK - 1) <= t) & (n < nc)
    ps = []
    for g in range(N_KV):
        sm = jnp.where(mask, ss[g], NEG)
        m = jnp.max(sm, axis=-1, keepdims=True)
        e = jnp.where(mask, jnp.exp(sm - m), 0.0)
        l = jnp.sum(e, axis=-1, keepdims=True)
        ps.append(e / jnp.where(l > 0.0, l, 1.0))
    ocs = [jnp.dot(ps[g].astype(BF), vcs[g], preferred_element_type=F32) for g in range(N_KV)]
    for g in range(N_KV):
        for h in range(HPG):
            c0 = (g * HPG + h) * HEAD_DIM
            oc_ref[:, c0:c0 + HEAD_DIM] = ocs[g][h * tq:(h + 1) * tq]
    nn = lax.broadcasted_iota(jnp.int32, (ncp, nsp), 0)
    bb = lax.broadcasted_iota(jnp.int32, (ncp, nsp), 1)
    dd = nn - bb * (SEL_BLOCK // CMP_STRIDE)
    ov = ((dd >= 1 - CMP_BLOCK // CMP_STRIDE) & (dd < SEL_BLOCK // CMP_STRIDE) & (nn < nc)).astype(BF)
    bcol = lax.broadcasted_iota(jnp.int32, (tq, nsp), 1)
    bcolf = bcol.astype(F32)
    tcol = q0 + lax.broadcasted_iota(jnp.int32, (tq, nsp), 0)
    causal = bcol * SEL_BLOCK <= tcol
    forced = (bcol == (tcol >> SEL_SHIFT)) | (bcol == 0)
    scores = []
    for g in range(N_KV):
        p = ps[g]
        imp = p[0:tq] + p[tq:2 * tq] + p[2 * tq:3 * tq] + p[3 * tq:4 * tq]
        hi = imp.astype(BF)
        r1 = imp - hi.astype(F32)
        mid = r1.astype(BF)
        lo = (r1 - mid.astype(F32)).astype(BF)
        imp_sel = (jnp.dot(hi, ov, preferred_element_type=F32) + jnp.dot(mid, ov, preferred_element_type=F32)
                   + jnp.dot(lo, ov, preferred_element_type=F32))
        score = jnp.where(causal, imp_sel + jnp.where(forced, FORCE_BONUS, 0.0), NEG)
        scores.append(jnp.where(bcol < ns, score, -3e38))
    sels = [jnp.zeros((tq, nsp), jnp.bool_) for _ in range(N_KV)]
    for _ in range(min(N_SEL, ns)):
        mxs = [jnp.max(scores[g], axis=-1, keepdims=True) for g in range(N_KV)]
        idxs = [jnp.min(jnp.where(scores[g] == mxs[g], bcolf, float(nsp)), axis=-1, keepdims=True)
                for g in range(N_KV)]
        for g in range(N_KV):
            pick = bcolf == idxs[g]
            sels[g] = sels[g] | pick
            scores[g] = jnp.where(pick, -jnp.inf, scores[g])
    for g in range(N_KV):
        sb_ref[:, g * nsp:(g + 1) * nsp] = jnp.where(sels[g], 0.0, NEG)


def _cmp_attn(u5, kcvc, *, nseq, tq_total, tq, row0, t_kv, pos0):
    ncp = kcvc.shape[3]
    nc = (t_kv - CMP_BLOCK) // CMP_STRIDE + 1
    ns = -(-t_kv // SEL_BLOCK)
    nsp = -(-ns // LANES) * LANES
    nqt = tq_total // tq
    rb0 = row0 // tq
    hd = N_HEADS * HEAD_DIM
    kern = functools.partial(_cmp_attn_kernel, tq=tq, nsp=nsp, ns=ns, nc=nc, pos0=pos0)
    return pl.pallas_call(
        kern,
        grid=(nseq, nqt),
        in_specs=[pl.BlockSpec((tq, hd), lambda b, i: (rb0 + b * nqt + i, 0)),
                  pl.BlockSpec((None, None, N_KV, ncp, HEAD_DIM), lambda b, i: (b, 0, 0, 0, 0)),
                  pl.BlockSpec((None, None, N_KV, ncp, HEAD_DIM), lambda b, i: (b, 1, 0, 0, 0))],
        out_specs=[pl.BlockSpec((tq, hd), lambda b, i: (b * nqt + i, 0)),
                   pl.BlockSpec((tq, N_KV * nsp), lambda b, i: (b * nqt + i, 0))],
        out_shape=[jax.ShapeDtypeStruct((nseq * tq_total, hd), F32),
                   jax.ShapeDtypeStruct((nseq * tq_total, N_KV * nsp), F32)],
        compiler_params=_cparams(("parallel", "arbitrary")),
        name="cmp_attn",
    )(u5, kcvc, kcvc)


EXP2_SCALE = SCALE * 1.4426950408889634


def _sel_win_kernel(q_ref, sb_ref, gt_ref, oc_ref, ks_ref, vs_ref, kw_ref, vw_ref, o_ref,
                    m_ref, acc_ref, s_ref, *, tq, nsp, tk):
    g = pl.program_id(1)
    q0 = pl.program_id(2) * tq
    rows = HPG * tq
    nt = (((1,), (1,)), ((), ()))
    q = _stack_heads(q_ref[...], tq).astype(BF)
    bias = jnp.concatenate([sb_ref[...]] * HPG, axis=0).astype(BF)
    qa = jnp.concatenate([q, bias], axis=1)
    ones = jnp.ones((tk, HEAD_DIM), BF)

    m_ref[...] = jnp.full(m_ref.shape, -jnp.inf, F32)
    acc_ref[...] = jnp.zeros(acc_ref.shape, F32)

    def scores(j):
        k0 = pl.multiple_of(j * tk, tk)
        blk = (k0 + lax.broadcasted_iota(jnp.int32, (tk, nsp), 0)) >> SEL_SHIFT
        onehot = (blk == lax.broadcasted_iota(jnp.int32, (tk, nsp), 1)).astype(BF)
        ka = jnp.concatenate([ks_ref[pl.ds(k0, tk), :], onehot], axis=1)
        return [lax.dot_general(qa[h * tq:(h + 1) * tq], ka, nt, preferred_element_type=F32) for h in range(HPG)]

    def consume(ss, j, masked):
        k0 = pl.multiple_of(j * tk, tk)
        va = jnp.concatenate([vs_ref[pl.ds(k0, tk), :], ones], axis=1)
        if masked:
            ok = (k0 + lax.broadcasted_iota(jnp.int32, (tq, tk), 1)
                  <= q0 + lax.broadcasted_iota(jnp.int32, (tq, tk), 0))
            ss = [jnp.where(ok, sh, NEG) for sh in ss]
        ps, alphas = [], []
        for h in range(HPG):
            rs = slice(h * tq, (h + 1) * tq)
            m_old = m_ref[rs]
            m_new = jnp.maximum(m_old, jnp.max(ss[h], axis=-1, keepdims=True))
            alphas.append(jnp.exp2((m_old - m_new) * EXP2_SCALE))
            ps.append(jnp.exp2((ss[h] - m_new) * EXP2_SCALE).astype(BF))
            m_ref[rs] = m_new
        for h in range(HPG):
            rs = slice(h * tq, (h + 1) * tq)
            acc_ref[rs] = alphas[h] * acc_ref[rs] + jnp.dot(ps[h], va, preferred_element_type=F32)

    j_last = (q0 + tq - 1) // tk
    for h, sh in enumerate(scores(0)):
        s_ref[h] = sh

    def sel_body(j, carry):
        cur = [s_ref[h] for h in range(HPG)]
        nxt = scores(j + 1)
        consume(cur, j, False)
        for h in range(HPG):
            s_ref[h] = nxt[h]
        return carry

    lax.fori_loop(0, j_last, sel_body, 0)
    nkw = WINDOW + tq
    w0 = pl.multiple_of(jnp.maximum(q0 - WINDOW, 0), tq)
    sw = lax.dot_general(q, kw_ref[pl.ds(w0, nkw), :], nt, preferred_element_type=F32)
    consume([s_ref[h] for h in range(HPG)], j_last, True)
    acc = acc_ref[...]
    o_s = acc[:, :HEAD_DIM] / acc[:, HEAD_DIM:HEAD_DIM + 1]
    vwa = jnp.concatenate([vw_ref[pl.ds(w0, nkw), :], jnp.ones((nkw, HEAD_DIM), BF)], axis=1)
    t = q0 + (lax.broadcasted_iota(jnp.int32, (rows, nkw), 0) & (tq - 1))
    kpos = w0 + lax.broadcasted_iota(jnp.int32, (rows, nkw), 1)
    sw = jnp.where((kpos <= t) & (kpos > t - WINDOW), sw, NEG)
    pw = jnp.exp2((sw - jnp.max(sw, axis=-1, keepdims=True)) * EXP2_SCALE)
    aw = jnp.dot(pw.astype(BF), vwa, preferred_element_type=F32)
    o_w = aw[:, :HEAD_DIM] / aw[:, HEAD_DIM:HEAD_DIM + 1]

    gts = gt_ref[...]
    oc = oc_ref[...]
    lane = lax.broadcasted_iota(jnp.int32, gts.shape, 1)
    for h in range(HPG):
        base = (g * HPG + h) * 3
        gc = jnp.sum(jnp.where(lane == base, gts, 0.0), axis=-1, keepdims=True)
        gs = jnp.sum(jnp.where(lane == base + 1, gts, 0.0), axis=-1, keepdims=True)
        gw = jnp.sum(jnp.where(lane == base + 2, gts, 0.0), axis=-1, keepdims=True)
        sl = slice(h * HEAD_DIM, (h + 1) * HEAD_DIM)
        rs = slice(h * tq, (h + 1) * tq)
        o_ref[:, sl] = gc * oc[:, sl] + gs * o_s[rs] + gw * o_w[rs]


def _sel_win_attn(u5, sbias, gates, oc, ksvs, kwvw, *, tq, tk):
    t = ksvs.shape[0]
    nsp = sbias.shape[1] // N_KV
    nqt = t // tq
    gw = HPG * HEAD_DIM
    assert t % tk == 0 and t % tq == 0 and t >= WINDOW + tq and WINDOW % tq == 0
    kern = functools.partial(_sel_win_kernel, tq=tq, nsp=nsp, tk=tk)
    return pl.pallas_call(
        kern,
        grid=(1, N_KV, nqt),
        in_specs=[pl.BlockSpec((tq, gw), lambda b, g, i: (i, g)),
                  pl.BlockSpec((tq, nsp), lambda b, g, i: (i, g)),
                  pl.BlockSpec((tq, LANES), lambda b, g, i: (i, 0)),
                  pl.BlockSpec((tq, gw), lambda b, g, i: (i, g)),
                  pl.BlockSpec((t, HEAD_DIM), lambda b, g, i: (0, g)),
                  pl.BlockSpec((t, HEAD_DIM), lambda b, g, i: (0, N_KV + g)),
                  pl.BlockSpec((t, HEAD_DIM), lambda b, g, i: (0, g)),
                  pl.BlockSpec((t, HEAD_DIM), lambda b, g, i: (0, N_KV + g))],
        out_specs=pl.BlockSpec((tq, gw), lambda b, g, i: (i, g)),
        out_shape=jax.ShapeDtypeStruct((t, N_KV * gw), F32),
        scratch_shapes=[pltpu.VMEM((HPG * tq, 1), F32), pltpu.VMEM((HPG * tq, 2 * HEAD_DIM), F32),
                        pltpu.VMEM((HPG, tq, tk), F32)],
        compiler_params=_cparams(("parallel", "arbitrary", "arbitrary")),
        name="sel_win_prompt",
    )(u5, sbias, gates, oc, ksvs, ksvs, kwvw, kwvw)


ROWS_PER_TOKEN = 4 * N_KV
CMP_PAGES_PER_STEP = 8


def _compress_paged_kernel(pt_ref, *refs, pg, page_size):
    page_refs = refs[:pg]
    w1_ref, o_ref, a_ref = refs[pg:]
    cpp = page_size // CMP_STRIDE
    for k in range(pg):
        for wg in range(2 * N_KV):
            w, g = divmod(wg, N_KV)
            r0 = (g * pg + k) * cpp
            for s in range(CMP_STRIDE):
                a_ref[w, r0:r0 + cpp, s * HEAD_DIM:(s + 1) * HEAD_DIM] = page_refs[k][
                    pl.ds(s, cpp, stride=CMP_STRIDE), wg, :]
    for w in range(2):
        p = jnp.dot(a_ref[w].astype(BF), w1_ref[w], preferred_element_type=F32)
        for g in range(N_KV):
            o_ref[w, g] = p[g * pg * cpp:(g + 1) * pg * cpp]


def _compress_paged(page_table, cache_tok, w1cat, *, page_size):
    bs, n_pages = page_table.shape
    pg = CMP_PAGES_PER_STEP
    assert n_pages % pg == 0
    cpp = page_size // CMP_STRIDE
    kd = CMP_STRIDE * HEAD_DIM
    page_spec = lambda k: pl.BlockSpec((page_size, 2 * N_KV, HEAD_DIM), lambda b, j, pt: (pt[b, j * pg + k], 0, 0))
    grid_spec = pltpu.PrefetchScalarGridSpec(
        num_scalar_prefetch=1,
        grid=(bs, n_pages // pg),
        in_specs=[page_spec(k) for k in range(pg)] + [pl.BlockSpec((2, kd, 2 * HEAD_DIM), lambda b, j, pt: (0, 0, 0))],
        out_specs=pl.BlockSpec((None, 2, N_KV, pg * cpp, 2 * HEAD_DIM), lambda b, j, pt: (b, 0, 0, j, 0)),
        scratch_shapes=[pltpu.VMEM((2, N_KV * pg * cpp, kd), F32)],
    )
    return pl.pallas_call(
        functools.partial(_compress_paged_kernel, pg=pg, page_size=page_size),
        grid_spec=grid_spec,
        out_shape=jax.ShapeDtypeStruct((bs, 2, N_KV, n_pages * cpp, 2 * HEAD_DIM), F32),
        compiler_params=_cparams(("parallel", "arbitrary")),
        name="compress_paged",
    )(page_table, *([cache_tok] * pg), w1cat)


def _compress_finish(p, pe_term, w2):
    nch = p.shape[0]
    pre = p[:, :HEAD_DIM] + pltpu.roll(p[:, HEAD_DIM:], nch - 1, 0) + pe_term
    return jnp.dot(jax.nn.gelu(pre).astype(BF), w2, preferred_element_type=F32).astype(BF)


def _cmp_attn_paged_kernel(q_ref, pk_ref, pv_ref, pe_ref, w1f_ref, w2_ref, oc_ref, sb_ref, *, tq, nsp, ns, nc, pos0):
    pe_k = jnp.dot(pe_ref[0].astype(BF), w1f_ref[0], preferred_element_type=F32)[0:1, :]
    pe_v = jnp.dot(pe_ref[1].astype(BF), w1f_ref[1], preferred_element_type=F32)[0:1, :]
    kcs = [_compress_finish(pk_ref[g], pe_k, w2_ref[0]) for g in range(N_KV)]
    vcs = [_compress_finish(pv_ref[g], pe_v, w2_ref[1]) for g in range(N_KV)]
    _cmp_attn_body(q_ref, kcs, vcs, oc_ref, sb_ref, pos0, tq=tq, nsp=nsp, ns=ns, nc=nc)


def _cmp_attn_paged(u5, pcmp, pe8, w1flat, w2, *, tq, row0, t_kv, pos0):
    bs, _, _, nch, _ = pcmp.shape
    nc = (t_kv - CMP_BLOCK) // CMP_STRIDE + 1
    ns = -(-t_kv // SEL_BLOCK)
    nsp = -(-ns // LANES) * LANES
    rb0 = row0 // tq
    hd = N_HEADS * HEAD_DIM
    kd = CMP_STRIDE * HEAD_DIM
    kern = functools.partial(_cmp_attn_paged_kernel, tq=tq, nsp=nsp, ns=ns, nc=nc, pos0=pos0)
    return pl.pallas_call(
        kern,
        grid=(bs,),
        in_specs=[pl.BlockSpec((tq, hd), lambda b: (rb0 + b, 0)),
                  pl.BlockSpec((None, None, N_KV, nch, 2 * HEAD_DIM), lambda b: (b, 0, 0, 0, 0)),
                  pl.BlockSpec((None, None, N_KV, nch, 2 * HEAD_DIM), lambda b: (b, 1, 0, 0, 0)),
                  pl.BlockSpec((2, SUBLANES, 2 * kd), lambda b: (0, 0, 0)),
                  pl.BlockSpec((2, 2 * kd, HEAD_DIM), lambda b: (0, 0, 0)),
                  pl.BlockSpec((2, HEAD_DIM, HEAD_DIM), lambda b: (0, 0, 0))],
        out_specs=[pl.BlockSpec((tq, hd), lambda b: (b, 0)),
                   pl.BlockSpec((tq, N_KV * nsp), lambda b: (b, 0))],
        out_shape=[jax.ShapeDtypeStruct((bs * tq, hd), F32),
                   jax.ShapeDtypeStruct((bs * tq, N_KV * nsp), F32)],
        compiler_params=_cparams(("parallel",)),
        name="cmp_attn_paged",
    )(u5, pcmp, pcmp, pe8, w1flat, w2)


SEL_PAGES_PER_STEP = 4


def _sel_win_paged_kernel(pt_ref, *refs, pps, n_pages, page_size, nsp, tq, win_keep, past_len):
    page_refs = refs[:pps]
    wq_ref, newkv_ref, wcache_ref, newwin_ref, gt_ref, oc_ref, o_ref, m_ref, l_ref, acc_ref = refs[pps:]
    j = pl.program_id(1)
    n_steps = n_pages // pps
    rows = N_HEADS * tq
    kvd = N_KV * HEAD_DIM
    wq = wq_ref[...]
    nt = (((1,), (1,)), ((), ()))

    @pl.when(j == 0)
    def _():
        m_ref[...] = jnp.full(m_ref.shape, -jnp.inf, F32)
        l_ref[...] = jnp.zeros(l_ref.shape, F32)
        acc_ref[...] = jnp.zeros(acc_ref.shape, F32)

    def tile(k_all, v_all, first_block, mask):
        nk = k_all.shape[0]
        blk = first_block + (lax.broadcasted_iota(jnp.int32, (nk, nsp), 0) >> SEL_SHIFT)
        onehot = (blk == lax.broadcasted_iota(jnp.int32, (nk, nsp), 1)).astype(BF)
        s = lax.dot_general(wq, jnp.concatenate([k_all, onehot], axis=1), nt, preferred_element_type=F32)
        if mask is not None:
            s = jnp.where(mask, s, NEG)
        m_old = m_ref[...]
        m_new = jnp.maximum(m_old, jnp.max(s, axis=-1, keepdims=True))
        alpha = jnp.exp2((m_old - m_new) * EXP2_SCALE)
        p = jnp.exp2((s - m_new) * EXP2_SCALE)
        l_ref[...] = alpha * l_ref[...] + jnp.sum(p, axis=-1, keepdims=True)
        acc_ref[...] = alpha * acc_ref[...] + jnp.dot(p.astype(BF), v_all, preferred_element_type=F32)
        m_ref[...] = m_new

    def page_rows(which):
        return jnp.concatenate(
            [jnp.concatenate([ref[:, which * N_KV + g, :] for g in range(N_KV)], axis=1) for ref in page_refs],
            axis=0).astype(BF)

    @pl.when(j < n_steps)
    def _():
        tile(page_rows(0), page_rows(1), j * (pps * page_size // SEL_BLOCK), None)

    @pl.when(j == n_steps)
    def _():
        qi = lax.broadcasted_iota(jnp.int32, (rows, page_size), 0) & (tq - 1)
        key = lax.broadcasted_iota(jnp.int32, (rows, page_size), 1)
        nk = newkv_ref[...]
        zpad = jnp.zeros((page_size - tq, kvd), F32)
        k_all = jnp.concatenate([nk[:, :kvd], zpad], axis=0).astype(BF)
        v_all = jnp.concatenate([nk[:, kvd:], zpad], axis=0).astype(BF)
        tile(k_all, v_all, n_pages * (page_size // SEL_BLOCK), key <= qi)
        o_s = acc_ref[...] / l_ref[...]

        nw = newwin_ref[...]
        kc = jnp.concatenate([wcache_ref[pl.ds(g, win_keep, stride=2 * N_KV), :] for g in range(N_KV)], axis=1)
        vc = jnp.concatenate([wcache_ref[pl.ds(N_KV + g, win_keep, stride=2 * N_KV), :] for g in range(N_KV)], axis=1)
        kw = jnp.concatenate([kc, nw[:, :kvd], zpad], axis=0).astype(BF)
        vw = jnp.concatenate([vc, nw[:, kvd:], zpad], axis=0).astype(BF)
        nkw = win_keep + page_size
        sw = lax.dot_general(wq[:, :kvd], kw, nt, preferred_element_type=F32)
        t = past_len + (lax.broadcasted_iota(jnp.int32, (rows, nkw), 0) & (tq - 1))
        kpos = (past_len - win_keep) + lax.broadcasted_iota(jnp.int32, (rows, nkw), 1)
        sw = jnp.where((kpos <= t) & (kpos > t - WINDOW), sw, NEG)
        pw = jnp.exp2((sw - jnp.max(sw, axis=-1, keepdims=True)) * EXP2_SCALE)
        o_w = jnp.dot(pw.astype(BF), vw, preferred_element_type=F32) / jnp.sum(pw, axis=-1, keepdims=True)

        gts = gt_ref[...]
        oc = oc_ref[...]
        lane = lax.broadcasted_iota(jnp.int32, gts.shape, 1)
        for hh in range(N_HEADS):
            g = hh // HPG
            gc = jnp.sum(jnp.where(lane == hh * 3, gts, 0.0), axis=-1, keepdims=True)
            gs = jnp.sum(jnp.where(lane == hh * 3 + 1, gts, 0.0), axis=-1, keepdims=True)
            gw = jnp.sum(jnp.where(lane == hh * 3 + 2, gts, 0.0), axis=-1, keepdims=True)
            rs = slice(hh * tq, (hh + 1) * tq)
            cs = slice(g * HEAD_DIM, (g + 1) * HEAD_DIM)
            os_ = slice(hh * HEAD_DIM, (hh + 1) * HEAD_DIM)
            o_ref[:, os_] = gc * oc[:, os_] + gs * o_s[rs, cs] + gw * o_w[rs, cs]


def _sel_win_paged(page_table, cache_tok, wq, u5, wcache_flat, gates, oc, *, tq, row0, page_size, win_keep, past_len):
    bs, n_pages = page_table.shape
    pps = SEL_PAGES_PER_STEP
    assert n_pages % pps == 0
    n_steps = n_pages // pps
    rows = N_HEADS * tq
    kvd = N_KV * HEAD_DIM
    nsp = wq.shape[2] - kvd
    rb0 = row0 // tq
    sel_col = (N_HEADS * HEAD_DIM + 2 * kvd) // (2 * kvd)
    kern = functools.partial(_sel_win_paged_kernel, pps=pps, n_pages=n_pages, page_size=page_size, nsp=nsp, tq=tq,
                             win_keep=win_keep, past_len=past_len)
    page_spec = lambda k: pl.BlockSpec(
        (page_size, 2 * N_KV, HEAD_DIM), lambda b, j, pt: (pt[b, jnp.minimum(j, n_steps - 1) * pps + k], 1, 0))
    grid_spec = pltpu.PrefetchScalarGridSpec(
        num_scalar_prefetch=1,
        grid=(bs, n_steps + 1),
        in_specs=[page_spec(k) for k in range(pps)] + [
            pl.BlockSpec((None, rows, kvd + nsp), lambda b, j, pt: (b, 0, 0)),
            pl.BlockSpec((tq, 2 * kvd), lambda b, j, pt: (rb0 + b, sel_col)),
            pl.BlockSpec((win_keep * 2 * N_KV, HEAD_DIM), lambda b, j, pt: (b, 0)),
            pl.BlockSpec((tq, 2 * kvd), lambda b, j, pt: (rb0 + b, sel_col + 1)),
            pl.BlockSpec((tq, LANES), lambda b, j, pt: (rb0 + b, 0)),
            pl.BlockSpec((tq, N_HEADS * HEAD_DIM), lambda b, j, pt: (b, 0))],
        out_specs=pl.BlockSpec((tq, N_HEADS * HEAD_DIM), lambda b, j, pt: (b, 0)),
        scratch_shapes=[pltpu.VMEM((rows, 1), F32), pltpu.VMEM((rows, 1), F32), pltpu.VMEM((rows, kvd), F32)],
    )
    return pl.pallas_call(
        kern,
        grid_spec=grid_spec,
        out_shape=jax.ShapeDtypeStruct((bs * tq, N_HEADS * HEAD_DIM), F32),
        compiler_params=_cparams(("parallel", "arbitrary")),
        name="sel_win_paged",
    )(page_table, *([cache_tok] * pps), wq, u5, wcache_flat, u5, gates, oc)


def _gelu(u):
    return jax.nn.gelu(u)


def _identity(u):
    return u


def kernel(x_prompt, x_sample, state_conv, state_h, cache_kv, cache_win, page_table, p_prompt, p_sample,
           lru_w_in, lru_conv_w, lru_conv_b, lru_w_a, lru_b_a, lru_w_x, lru_b_x, lru_lambda, lru_w_out,
           nsa_w_in, nsa_cmp_w1, nsa_cmp_pe, nsa_cmp_w2, nsa_w_out,
           ln1_g, ln1_b, ln2_g, ln2_b,
           moe_w_group, moe_b_group, moe_w_expert, moe_b_expert, moe_w_gate, moe_w_up, moe_w_down,
           ple_w_gate, ple_w_proj):
    bp, tp, d = x_prompt.shape
    bs, ts, _ = x_sample.shape
    assert bp == 1 and ts == SUBLANES
    n_p = bp * tp
    n_s = bs * ts
    n = n_p + n_s
    past_len = page_table.shape[1] * cache_kv.shape[2]
    d_rnn = lru_w_out.shape[1]
    hd = N_HEADS * HEAD_DIM
    kvw = 2 * N_KV * HEAD_DIM

    x0 = jnp.concatenate([x_prompt.reshape(n_p, d), x_sample.reshape(n_s, d)], axis=0)

    def moe_and_ple(x1, layer):
        w_route = jnp.zeros((d, LANES), F32).at[:, :N_EXPERTS].set(moe_w_expert[layer])
        w_route = w_route.at[:, N_EXPERTS:N_EXPERTS + N_GROUPS].set(moe_w_group[layer])
        b_route = jnp.zeros((1, LANES), F32).at[0, :N_EXPERTS].set(moe_b_expert[layer])
        b_route = b_route.at[0, N_EXPERTS:N_EXPERTS + N_GROUPS].set(moe_b_group[layer])
        y0, y1 = _moe_layer(x1, w_route, b_route, moe_w_gate, moe_w_up, moe_w_down, layer)
        p_cat = jnp.concatenate([p_prompt[layer].reshape(n_p, -1), p_sample[layer].reshape(n_s, -1)], axis=0)
        return _ln_ple(x1, y0, y1, ln2_g[layer].reshape(1, d), ln2_b[layer].reshape(1, d),
                       ple_w_gate[layer].astype(BF), p_cat, ple_w_proj[layer].astype(BF))

    w_in = lru_w_in[0].astype(BF)
    x0b = x0.astype(BF)
    gate = _mm_act(x0b, w_in[:, :d_rnn], _gelu)
    xr = _mm_act(x0b, w_in[:, d_rnn:], _identity)
    lru_args = (lru_conv_w[0], lru_conv_b[0].reshape(1, d_rnn), lru_w_a[0].astype(BF), lru_b_a[0].reshape(1, d_rnn),
                lru_w_x[0].astype(BF), lru_b_x[0].reshape(1, d_rnn), lru_lambda[0].reshape(1, d_rnn))
    hg_p, hl_p = _rglru_core(xr, gate, jnp.zeros((1, SUBLANES, d_rnn), F32), jnp.zeros((1, 1, d_rnn), F32),
                             *lru_args, chain=True, tt=_row_tile(n_p, 512), row0=0, n=n_p)
    prev_s = jnp.pad(state_conv[0], ((0, 0), (SUBLANES - (CONV_W - 1), 0), (0, 0)))
    hg_s, hl_s = _rglru_core(xr, gate, prev_s, state_h[0].reshape(bs, 1, d_rnn),
                             *lru_args, chain=False, tt=n_s, row0=n_p, n=n_s)
    hg = jnp.concatenate([hg_p, hg_s], axis=0)
    conv_prompt = xr[n_p - (CONV_W - 1):n_p].reshape(1, bp, CONV_W - 1, d_rnn)
    h_prompt = hl_p[n_p // SUBLANES - 1].reshape(1, bp, d_rnn)
    conv_sample = xr[n_p:].reshape(bs, ts, d_rnn)[:, ts - (CONV_W - 1):].reshape(1, bs, CONV_W - 1, d_rnn)
    h_sample = hl_s.reshape(1, bs, d_rnn)
    x1 = _mm_res_ln(hg, lru_w_out[0].astype(BF), x0, ln1_g[0].reshape(1, d), ln1_b[0].reshape(1, d))
    xa = moe_and_ple(x1, 0)

    w_nsa = nsa_w_in[0]
    n_main = hd + 3 * kvw
    pos = jnp.concatenate([jnp.arange(n_p, dtype=jnp.int32) % tp,
                           past_len + jnp.arange(n_s, dtype=jnp.int32) % ts])
    half = HEAD_DIM // 2
    inv = ROPE_THETA ** (-jnp.arange(half, dtype=F32) / half)
    ang = pos.astype(F32)[:, None] * inv[None, :]
    cos = jnp.concatenate([jnp.cos(ang), jnp.cos(ang)], axis=1)
    sin_signed = jnp.concatenate([-jnp.sin(ang), jnp.sin(ang)], axis=1)
    xab = xa.astype(BF)
    u5, ukv = _nsa_in(xab, w_nsa[:, :n_main].astype(BF), cos, sin_signed)
    w_gates = jnp.zeros((d, LANES), F32).at[:, :3 * N_HEADS].set(w_nsa[:, n_main:]).astype(BF)
    gates = _mm_act(xab, w_gates, jax.nn.sigmoid)

    kv_prompt = u5[:n_p, hd:hd + 2 * kvw].reshape(1, bp, tp, 4, N_KV, HEAD_DIM)
    kv_sample = u5[n_p:, hd:hd + 2 * kvw].reshape(1, bs, ts, 4, N_KV, HEAD_DIM)
    win_new_p = u5[:n_p, hd + 2 * kvw:]
    win_new_s = u5[n_p:, hd + 2 * kvw:].reshape(bs, ts, kvw)
    win_keep_p = min(WINDOW, tp)
    win_prompt = win_new_p[n_p - win_keep_p:].reshape(1, bp, win_keep_p, 2, N_KV, HEAD_DIM)
    win_keep_s = cache_win.shape[2]
    win_sample = jnp.concatenate([cache_win[0][:, ts:], win_new_s.reshape(bs, ts, 2, N_KV, HEAD_DIM)], axis=1)[None]

    kd = CMP_STRIDE * HEAD_DIM
    w1 = nsa_cmp_w1[0]
    w1flat = w1.reshape(2, 2 * kd, HEAD_DIM).astype(BF)
    w1cat = jnp.concatenate([w1flat[:, :kd], w1flat[:, kd:]], axis=2)
    pe8 = jnp.zeros((2, SUBLANES, 2 * kd), F32).at[:, 0].set(nsa_cmp_pe[0].reshape(2, 2 * kd))
    w2 = nsa_cmp_w2[0].astype(BF)

    kcvc_p = _compress(u5[:n_p, hd:hd + kvw].reshape(bp, tp, kvw), w1cat, pe8, w1flat, w2)
    tq_p = _row_tile(tp, 128)
    oc_p, sb_p = _cmp_attn(u5, kcvc_p, nseq=bp, tq_total=tp, tq=tq_p, row0=0, t_kv=tp, pos0=0)
    ksvs_p = ukv[:n_p, :kvw]
    kwvw_p = ukv[:n_p, kvw:]
    o_p = _sel_win_attn(u5, sb_p, gates, oc_p, ksvs_p, kwvw_p, tq=_row_tile(tp, 256), tk=_row_tile(tp, 1024))

    page_size = cache_kv.shape[2]
    cache_tok = cache_kv[0].reshape(-1, ROWS_PER_TOKEN, HEAD_DIM)
    pcmp = _compress_paged(page_table, cache_tok, w1cat, page_size=page_size)
    t_kv_s = past_len + ts
    oc_s, sb_s = _cmp_attn_paged(u5, pcmp, pe8, w1flat, w2, tq=ts, row0=n_p, t_kv=t_kv_s, pos0=past_len)
    nsp_s = sb_s.shape[1] // N_KV
    q5 = u5[n_p:, :hd].reshape(bs, ts, N_KV, HPG, HEAD_DIM)
    eye = jnp.eye(N_KV, dtype=F32)[None, None, :, None, :, None]
    q_diag = (q5[:, :, :, :, None, :] * eye).transpose(0, 2, 3, 1, 4, 5).reshape(bs, N_HEADS * ts, N_KV * HEAD_DIM)
    bias = jnp.broadcast_to(sb_s.reshape(bs, ts, N_KV, 1, nsp_s).transpose(0, 2, 3, 1, 4),
                            (bs, N_KV, HPG, ts, nsp_s)).reshape(bs, N_HEADS * ts, nsp_s)
    wq = jnp.concatenate([q_diag, bias], axis=2).astype(BF)
    o_s = _sel_win_paged(page_table, cache_tok, wq, u5, cache_win[0].reshape(-1, HEAD_DIM), gates, oc_s,
                         tq=ts, row0=n_p, page_size=page_size, win_keep=win_keep_s, past_len=past_len)

    o_all = jnp.concatenate([o_p, o_s], axis=0)
    x1 = _mm_res_ln(o_all, nsa_w_out[0].astype(BF), xa, ln1_g[1].reshape(1, d), ln1_b[1].reshape(1, d))
    xo = moe_and_ple(x1, 1)

    y_prompt = xo[:n_p].reshape(bp, tp, d)
    y_sample = xo[n_p:].reshape(bs, ts, d)
    return (y_prompt, y_sample, conv_prompt, h_prompt, kv_prompt, win_prompt,
            conv_sample, h_sample, kv_sample, win_sample)
```

```python
import functools

import jax
import jax.numpy as jnp
from jax import lax
from jax.experimental import pallas as pl
from jax.experimental.pallas import tpu as pltpu

F32 = jnp.float32
BF = jnp.bfloat16

HEAD_DIM = 128
N_HEADS = 16
N_KV = 4
HPG = N_HEADS // N_KV
LRU_BLOCK = 256
CONV_W = 4
LRU_C = 8.0
CMP_BLOCK = 32
CMP_STRIDE = 16
SEL_BLOCK = 64
N_SEL = 16
WINDOW = 512
ROPE_THETA = 10000.0
N_GROUPS = 4
EPG = 8
N_EXPERTS = N_GROUPS * EPG
DEPTH = 2
ALPHA = (2 * DEPTH) ** 0.25
LN_EPS = 1e-5
NEG = -1e30
FORCE_BONUS = 1e4
SCALE = HEAD_DIM ** -0.5
SEL_SHIFT = SEL_BLOCK.bit_length() - 1
EPG_SHIFT = EPG.bit_length() - 1

LANES = 128
SUBLANES = 8
VMEM_LIMIT = 56 * 1024 * 1024
MOE_TM = 1024
MOE_SUB = 256
MOE_TF = 256


def _cparams(sem):
    return pltpu.CompilerParams(dimension_semantics=sem, vmem_limit_bytes=VMEM_LIMIT)


def _row_tile(n, cap):
    best = None
    for t in range(SUBLANES, cap + 1, SUBLANES):
        if n % t == 0:
            best = t
    assert best is not None, (n, cap)
    return best


def _mm_act_kernel(x_ref, w_ref, o_ref, *, act):
    u = jnp.dot(x_ref[...].astype(BF), w_ref[...], preferred_element_type=F32)
    o_ref[...] = act(u).astype(o_ref.dtype)


def _mm_act(x, w, act, out_dtype=F32, tn=512):
    m, k = x.shape
    n = w.shape[1]
    tm = _row_tile(m, 768)
    tn = min(tn, n)
    assert n % tn == 0
    return pl.pallas_call(
        functools.partial(_mm_act_kernel, act=act),
        grid=(m // tm, n // tn),
        in_specs=[pl.BlockSpec((tm, k), lambda i, j: (i, 0)),
                  pl.BlockSpec((k, tn), lambda i, j: (0, j))],
        out_specs=pl.BlockSpec((tm, tn), lambda i, j: (i, j)),
        out_shape=jax.ShapeDtypeStruct((m, n), out_dtype),
        compiler_params=_cparams(("parallel", "arbitrary")),
        name="mm_act",
    )(x, w)


def _rglru_kernel(xr_ref, prev_ref, h0_ref, gate_ref, cw_ref, cb_ref, wa_ref, ba_ref, wx_ref, bx_ref,
                  lam_ref, hg_ref, hlast_ref, carry_prev, carry_h, h_scr, *, chain):
    tt, c = xr_ref.shape
    g = tt // SUBLANES
    x = xr_ref[...].reshape(g, SUBLANES, c)
    if chain:
        first = pl.program_id(1) == 0
        p0 = jnp.where(first, prev_ref[...], carry_prev[...])
        prev = jnp.concatenate([p0, x[:-1]], axis=0) if g > 1 else p0
        h_start = jnp.where(first, h0_ref[...], carry_h[...])
    else:
        prev = prev_ref[...]
    sub = lax.broadcasted_iota(jnp.int32, (g, SUBLANES, c), 1)
    cw = [cw_ref[k:k + 1, :].reshape(1, 1, c) for k in range(CONV_W)]
    xc = cb_ref[...].reshape(1, 1, c) + x * cw[CONV_W - 1]
    for s in range(1, CONV_W):
        sh = jnp.where(sub < s, pltpu.roll(prev, s, 1), pltpu.roll(x, s, 1))
        xc = xc + sh * cw[CONV_W - 1 - s]
    xc2 = xc.reshape(tt, c)
    xb = xc2.astype(BF)
    r = jax.nn.sigmoid(jnp.dot(xb, wa_ref[...], preferred_element_type=F32) + ba_ref[...])
    i = jax.nn.sigmoid(jnp.dot(xb, wx_ref[...], preferred_element_type=F32) + bx_ref[...])
    log_a = (-LRU_C * r) * jax.nn.softplus(-lam_ref[...])
    a = jnp.exp(log_a)
    bterm = jnp.sqrt(-jnp.tanh(log_a) * (a * a + 1.0)) * (i * xc2)
    a3 = a.reshape(g, SUBLANES, c)
    b3 = bterm.reshape(g, SUBLANES, c)
    for d in (1, 2, 4):
        a_sh = jnp.where(sub < d, 1.0, pltpu.roll(a3, d, 1))
        b_sh = jnp.where(sub < d, 0.0, pltpu.roll(b3, d, 1))
        b3 = a3 * b_sh + b3
        a3 = a3 * a_sh
    if chain:
        hs = []
        hprev = h_start.reshape(1, c)
        for j in range(g):
            hj = a3[j] * hprev + b3[j]
            hs.append(hj)
            hprev = hj[SUBLANES - 1:SUBLANES, :]
        h3 = jnp.stack(hs, axis=0)
        carry_prev[...] = x[g - 1:g]
        carry_h[...] = hprev.reshape(1, 1, c)
    else:
        h3 = a3 * h0_ref[...] + b3
    h2 = h3.reshape(tt, c)
    hg_ref[...] = (h2 * gate_ref[...]).astype(hg_ref.dtype)
    for k in range(c // LANES):
        h_scr[k] = h2[:, k * LANES:(k + 1) * LANES]
        hlast_ref[:, k * LANES:(k + 1) * LANES] = h_scr[k, pl.ds(SUBLANES - 1, g, stride=SUBLANES), :]


def _rglru_core(xr, gate, prev, h0, cw, cb, wa, ba, wx, bx, lam, *, chain, tt, row0, n):
    d = xr.shape[1]
    rb0 = row0 // tt
    assert row0 % tt == 0 and n % tt == 0
    c = LRU_BLOCK
    g = tt // SUBLANES
    pg = 1 if chain else g
    vec = lambda: pl.BlockSpec((1, c), lambda nb, t: (0, nb))
    return pl.pallas_call(
        functools.partial(_rglru_kernel, chain=chain),
        grid=(d // c, n // tt),
        in_specs=[pl.BlockSpec((tt, c), lambda nb, t: (rb0 + t, nb)),
                  pl.BlockSpec((pg, SUBLANES, c), lambda nb, t: (0 if chain else t, 0, nb)),
                  pl.BlockSpec((pg, 1, c), lambda nb, t: (0 if chain else t, 0, nb)),
                  pl.BlockSpec((tt, c), lambda nb, t: (rb0 + t, nb)),
                  pl.BlockSpec((CONV_W, c), lambda nb, t: (0, nb)),
                  vec(),
                  pl.BlockSpec((None, c, c), lambda nb, t: (nb, 0, 0)),
                  vec(),
                  pl.BlockSpec((None, c, c), lambda nb, t: (nb, 0, 0)),
                  vec(), vec()],
        out_specs=[pl.BlockSpec((tt, c), lambda nb, t: (t, nb)),
                   pl.BlockSpec((g, c), lambda nb, t: (t, nb))],
        out_shape=[jax.ShapeDtypeStruct((n, d), BF), jax.ShapeDtypeStruct((n // SUBLANES, d), F32)],
        scratch_shapes=[pltpu.VMEM((1, SUBLANES, c), F32), pltpu.VMEM((1, 1, c), F32),
                        pltpu.VMEM((c // LANES, tt, LANES), F32)],
        compiler_params=_cparams(("parallel", "arbitrary")),
        name="rglru_core",
    )(xr, prev, h0, gate, cw, cb, wa, ba, wx, bx, lam)


def _layer_norm(z, g, b):
    mu = jnp.mean(z, axis=-1, keepdims=True)
    zc = z - mu
    var = jnp.mean(zc * zc, axis=-1, keepdims=True)
    return zc * lax.rsqrt(var + LN_EPS) * g + b


def _mm_res_ln_kernel(a_ref, w_ref, res_ref, g_ref, b_ref, o_ref):
    y = jnp.dot(a_ref[...].astype(BF), w_ref[...], preferred_element_type=F32)
    o_ref[...] = _layer_norm(ALPHA * res_ref[...] + y, g_ref[...], b_ref[...])


def _mm_res_ln(a, w, res, g, b):
    m, k = a.shape
    d = w.shape[1]
    tm = _row_tile(m, 256)
    return pl.pallas_call(
        _mm_res_ln_kernel,
        grid=(m // tm,),
        in_specs=[pl.BlockSpec((tm, k), lambda i: (i, 0)),
                  pl.BlockSpec((k, d), lambda i: (0, 0)),
                  pl.BlockSpec((tm, d), lambda i: (i, 0)),
                  pl.BlockSpec((1, d), lambda i: (0, 0)),
                  pl.BlockSpec((1, d), lambda i: (0, 0))],
        out_specs=pl.BlockSpec((tm, d), lambda i: (i, 0)),
        out_shape=jax.ShapeDtypeStruct((m, d), F32),
        compiler_params=_cparams(("parallel",)),
        name="mm_res_ln",
    )(a, w, res, g, b)


def _route_kernel(x_ref, w_ref, b_ref, o_ref):
    logits = jnp.dot(x_ref[...], w_ref[...], preferred_element_type=F32,
                     precision=lax.Precision.HIGHEST) + b_ref[...]
    tm = logits.shape[0]
    lane = lax.broadcasted_iota(jnp.int32, (tm, LANES), 1)
    lanef = lane.astype(F32)
    big = float(LANES)
    ninf = -jnp.inf
    is_grp = (lane >= N_EXPERTS) & (lane < N_EXPERTS + N_GROUPS)
    lg = jnp.where(is_grp, logits, ninf)
    mg = jnp.max(lg, axis=-1, keepdims=True)
    p_top = 1.0 / jnp.sum(jnp.where(is_grp, jnp.exp(lg - mg), 0.0), axis=-1, keepdims=True)
    g_sel = jnp.min(jnp.where(lg == mg, lanef, big), axis=-1, keepdims=True) - float(N_EXPERTS)
    in_grp = (lane < N_EXPERTS) & ((lane >> EPG_SHIFT).astype(F32) == g_sel)
    le = jnp.where(in_grp, logits, ninf)
    v1 = jnp.max(le, axis=-1, keepdims=True)
    j1 = jnp.min(jnp.where(le == v1, lanef, big), axis=-1, keepdims=True)
    le2 = jnp.where(lanef == j1, ninf, le)
    v2 = jnp.max(le2, axis=-1, keepdims=True)
    j2 = jnp.min(jnp.where(le2 == v2, lanef, big), axis=-1, keepdims=True)
    e2 = jnp.exp(v2 - v1)
    den = 1.0 + e2
    w1 = (1.0 / den) * p_top
    w2 = (e2 / den) * p_top
    out = jnp.where(lane == 0, j1, jnp.where(lane == 1, j2, jnp.where(lane == 2, w1, jnp.where(lane == 3, w2, 0.0))))
    o_ref[...] = out


def _route(x, w_cat, b_cat):
    m, d = x.shape
    tm = _row_tile(m, 256)
    return pl.pallas_call(
        _route_kernel,
        grid=(m // tm,),
        in_specs=[pl.BlockSpec((tm, d), lambda i: (i, 0)),
                  pl.BlockSpec((d, LANES), lambda i: (0, 0)),
                  pl.BlockSpec((1, LANES), lambda i: (0, 0))],
        out_specs=pl.BlockSpec((tm, LANES), lambda i: (i, 0)),
        out_shape=jax.ShapeDtypeStruct((m, LANES), F32),
        compiler_params=_cparams(("parallel",)),
        name="moe_route",
    )(x, w_cat, b_cat)


def _moe_kernel(te_ref, ns_ref, st_ref, tok_ref, x_hbm, wt_ref, wg_ref, wu_ref, wd_ref, o_ref, xbuf, sem):
    i = pl.program_id(0)
    f = pl.program_id(1)
    nsub = ns_ref[i]

    def row_copy(tok, r):
        return pltpu.make_async_copy(x_hbm.at[pl.ds(tok, 1)], xbuf.at[pl.ds(r, 1)], sem)

    @pl.when(f == 0)
    def _():
        o_ref[...] = jnp.zeros_like(o_ref)
        base = st_ref[i] * MOE_SUB
        nrows = nsub * MOE_SUB

        def issue(r, carry):
            row_copy(tok_ref[base + r], r).start()
            return carry

        def drain(r, carry):
            row_copy(0, r).wait()
            return carry

        lax.fori_loop(0, nrows, issue, 0)
        lax.fori_loop(0, nrows, drain, 0)

    @pl.when(nsub > 0)
    def _():
        wg = wg_ref[...].astype(BF)
        wu = wu_ref[...].astype(BF)
        wd = wd_ref[...].astype(BF)

        def body(s, carry):
            r0 = pl.multiple_of(s * MOE_SUB, MOE_SUB)
            xs = xbuf[pl.ds(r0, MOE_SUB), :].astype(BF)
            hg = jnp.dot(xs, wg, preferred_element_type=F32)
            hu = jnp.dot(xs, wu, preferred_element_type=F32)
            h = (jax.nn.silu(hg) * hu) * wt_ref[pl.ds(r0, MOE_SUB), :]
            o_ref[pl.ds(r0, MOE_SUB), :] += jnp.dot(h.astype(BF), wd, preferred_element_type=F32)
            return carry

        lax.fori_loop(0, nsub, body, 0)


def _moe_ffn(tile_expert, tile_nsub, tile_start, tok_pad, x, wt, w_gate, w_up, w_down, layer):
    _, d = x.shape
    nt = tile_expert.shape[0]
    ff = w_gate.shape[3]
    grid_spec = pltpu.PrefetchScalarGridSpec(
        num_scalar_prefetch=4,
        grid=(nt, ff // MOE_TF),
        in_specs=[pl.BlockSpec(memory_space=pl.ANY),
                  pl.BlockSpec((pl.Element(MOE_TM), pl.Element(1)), lambda i, f, te, ns, st, tk: (st[i] * MOE_SUB, 0)),
                  pl.BlockSpec((None, None, d, MOE_TF), lambda i, f, te, ns, st, tk: (layer, te[i], 0, f)),
                  pl.BlockSpec((None, None, d, MOE_TF), lambda i, f, te, ns, st, tk: (layer, te[i], 0, f)),
                  pl.BlockSpec((None, None, MOE_TF, d), lambda i, f, te, ns, st, tk: (layer, te[i], f, 0))],
        out_specs=pl.BlockSpec((MOE_TM, d), lambda i, f, te, ns, st, tk: (i, 0)),
        scratch_shapes=[pltpu.VMEM((MOE_TM, d), F32), pltpu.SemaphoreType.DMA(())],
    )
    return pl.pallas_call(
        _moe_kernel,
        grid_spec=grid_spec,
        out_shape=jax.ShapeDtypeStruct((nt * MOE_TM, d), F32),
        compiler_params=_cparams(("arbitrary", "arbitrary")),
        name="moe_ffn",
    )(tile_expert, tile_nsub, tile_start, tok_pad, x, wt, w_gate, w_up, w_down)


def _ln_ple_kernel(x1_ref, y0_ref, y1_ref, g_ref, b_ref, wg_ref, p_ref, wp_ref, o_ref):
    z = ALPHA * x1_ref[...] + (y0_ref[...] + y1_ref[...])
    x2 = _layer_norm(z, g_ref[...], b_ref[...])
    gate = jax.nn.sigmoid(jnp.dot(x2.astype(BF), wg_ref[...], preferred_element_type=F32))
    proj = jnp.dot(p_ref[...].astype(BF), wp_ref[...], preferred_element_type=F32)
    o_ref[...] = x2 + gate * proj


def _ln_ple(x1, y0, y1, g, b, w_gate, p, w_proj):
    m, d = x1.shape
    pd = p.shape[1]
    tm = _row_tile(m, 256)
    row = lambda w: pl.BlockSpec((tm, w), lambda i: (i, 0))
    return pl.pallas_call(
        _ln_ple_kernel,
        grid=(m // tm,),
        in_specs=[row(d), row(d), row(d),
                  pl.BlockSpec((1, d), lambda i: (0, 0)),
                  pl.BlockSpec((1, d), lambda i: (0, 0)),
                  pl.BlockSpec((d, d), lambda i: (0, 0)),
                  row(pd),
                  pl.BlockSpec((pd, d), lambda i: (0, 0))],
        out_specs=row(d),
        out_shape=jax.ShapeDtypeStruct((m, d), F32),
        compiler_params=_cparams(("parallel",)),
        name="ln_ple",
    )(x1, y0, y1, g, b, w_gate, p, w_proj)


def _moe_layer(x1, w_route, b_route, w_gate, w_up, w_down, layer):
    n, d = x1.shape
    routed = _route(x1, w_route, b_route)
    e_tok = routed[:, 0:2].astype(jnp.int32)
    w_tok = routed[:, 2:4]
    a = 2 * n
    sub_per_tile = MOE_TM // MOE_SUB
    nt = a // MOE_TM + N_EXPERTS
    n_sub_max = a // MOE_SUB + N_EXPERTS
    e_flat = e_tok.reshape(a)
    onehot = (e_flat[:, None] == jnp.arange(N_EXPERTS, dtype=jnp.int32)[None, :]).astype(jnp.int32)
    csum = jnp.cumsum(onehot, axis=0)
    counts = csum[a - 1]
    rank = jnp.sum(csum * onehot, axis=1) - 1
    sub_per = (counts + MOE_SUB - 1) // MOE_SUB
    sub_off = jnp.cumsum(sub_per) - sub_per
    tiles_per = (counts + MOE_TM - 1) // MOE_TM
    tile_end = jnp.cumsum(tiles_per)
    tile_off = tile_end - tiles_per
    src_row = sub_off[e_flat] * MOE_SUB + rank
    dst_row = (tile_off[e_flat] + rank // MOE_TM) * MOE_TM + rank % MOE_TM
    rows_in = (n_sub_max + sub_per_tile) * MOE_SUB
    tok_pad = jnp.zeros((rows_in,), jnp.int32).at[src_row].set(jnp.arange(a, dtype=jnp.int32) // 2)
    w_pad = jnp.zeros((rows_in,), F32).at[src_row].set(w_tok.reshape(a))
    tile_id = jnp.arange(nt, dtype=jnp.int32)
    n_used = tile_end[N_EXPERTS - 1]
    tile_expert = jnp.minimum(jnp.sum((tile_end[None, :] <= tile_id[:, None]).astype(jnp.int32), axis=1),
                              N_EXPERTS - 1)
    k_in_expert = tile_id - tile_off[tile_expert]
    used = tile_id < n_used
    tile_nsub = jnp.where(used, jnp.clip(sub_per[tile_expert] - k_in_expert * sub_per_tile, 0, sub_per_tile), 0)
    tile_start = jnp.where(used, sub_off[tile_expert] + k_in_expert * sub_per_tile, 0)
    last_used = jnp.sum(jnp.where(tile_id == n_used - 1, tile_expert, 0))
    tile_expert = jnp.where(used, tile_expert, last_used)
    yg = _moe_ffn(tile_expert.astype(jnp.int32), tile_nsub.astype(jnp.int32), tile_start.astype(jnp.int32),
                  tok_pad, x1, w_pad.reshape(-1, 1), w_gate, w_up, w_down, layer)
    dst2 = dst_row.reshape(n, 2)
    return (yg.at[dst2[:, 0]].get(mode="promise_in_bounds"), yg.at[dst2[:, 1]].get(mode="promise_in_bounds"))


def _nsa_in_kernel(x_ref, w_ref, cos_ref, sin_ref, o_ref, ob_ref, *, n_q_blocks):
    j = pl.program_id(1)
    u = jnp.dot(x_ref[...].astype(BF), w_ref[...], preferred_element_type=F32)
    cos = cos_ref[...]
    sin = sin_ref[...]
    heads = u.shape[1] // HEAD_DIM
    for h in range(heads):
        seg = u[:, h * HEAD_DIM:(h + 1) * HEAD_DIM]
        rot = seg * cos + pltpu.roll(seg, HEAD_DIM // 2, 1) * sin
        if h < heads // 2:
            val = rot
        else:
            val = jnp.where(j < n_q_blocks, rot, seg)
        o_ref[:, h * HEAD_DIM:(h + 1) * HEAD_DIM] = val

    @pl.when(j > n_q_blocks)
    def _():
        ob_ref[...] = o_ref[...].astype(BF)


def _nsa_in(xb, w, cos, sin_signed):
    m, k = xb.shape
    n = w.shape[1]
    tn = 2 * N_KV * HEAD_DIM
    tm = _row_tile(m, 768)
    nqb = (N_HEADS * HEAD_DIM) // tn
    return pl.pallas_call(
        functools.partial(_nsa_in_kernel, n_q_blocks=nqb),
        grid=(m // tm, n // tn),
        in_specs=[pl.BlockSpec((tm, k), lambda i, j: (i, 0)),
                  pl.BlockSpec((k, tn), lambda i, j: (0, j)),
                  pl.BlockSpec((tm, HEAD_DIM), lambda i, j: (i, 0)),
                  pl.BlockSpec((tm, HEAD_DIM), lambda i, j: (i, 0))],
        out_specs=[pl.BlockSpec((tm, tn), lambda i, j: (i, j)),
                   pl.BlockSpec((tm, tn), lambda i, j: (i, jnp.maximum(j - (nqb + 1), 0)))],
        out_shape=[jax.ShapeDtypeStruct((m, n), F32), jax.ShapeDtypeStruct((m, n - (nqb + 1) * tn), BF)],
        compiler_params=_cparams(("parallel", "arbitrary")),
        name="nsa_in",
    )(xb, w, cos, sin_signed)


def _compress_kernel(rows_ref, w1_ref, pe_ref, w1f_ref, w2_ref, o_ref, a_ref):
    t = rows_ref.shape[0]
    nch = t // CMP_STRIDE
    for s in range(CMP_STRIDE):
        a_ref[:, s * HEAD_DIM:(s + 1) * HEAD_DIM] = rows_ref[pl.ds(s, nch, stride=CMP_STRIDE), :].astype(BF)
    p = jnp.dot(a_ref[...], w1_ref[...], preferred_element_type=F32)
    pe_term = jnp.dot(pe_ref[...].astype(BF), w1f_ref[...], preferred_element_type=F32)[0:1, :]
    pre = p[:, :HEAD_DIM] + pltpu.roll(p[:, HEAD_DIM:], nch - 1, 0) + pe_term
    o_ref[...] = jnp.dot(jax.nn.gelu(pre).astype(BF), w2_ref[...], preferred_element_type=F32).astype(o_ref.dtype)


def _compress(rows, w1cat, pe8, w1flat, w2):
    nseq, t, _ = rows.shape
    nch = t // CMP_STRIDE
    kd = CMP_STRIDE * HEAD_DIM
    return pl.pallas_call(
        _compress_kernel,
        grid=(nseq, 2, N_KV),
        in_specs=[pl.BlockSpec((None, t, HEAD_DIM), lambda b, w, g: (b, 0, w * N_KV + g)),
                  pl.BlockSpec((None, kd, 2 * HEAD_DIM), lambda b, w, g: (w, 0, 0)),
                  pl.BlockSpec((None, SUBLANES, 2 * kd), lambda b, w, g: (w, 0, 0)),
                  pl.BlockSpec((None, 2 * kd, HEAD_DIM), lambda b, w, g: (w, 0, 0)),
                  pl.BlockSpec((None, HEAD_DIM, HEAD_DIM), lambda b, w, g: (w, 0, 0))],
        out_specs=pl.BlockSpec((None, None, None, nch, HEAD_DIM), lambda b, w, g: (b, w, g, 0, 0)),
        out_shape=jax.ShapeDtypeStruct((nseq, 2, N_KV, nch, HEAD_DIM), BF),
        scratch_shapes=[pltpu.VMEM((nch, kd), BF)],
        compiler_params=_cparams(("parallel", "arbitrary", "arbitrary")),
        name="compress",
    )(rows, w1cat, pe8, w1flat, w2)


def _stack_heads(qblk, tq):
    return jnp.concatenate([qblk[:, h * HEAD_DIM:(h + 1) * HEAD_DIM] for h in range(HPG)], axis=0)


def _cmp_attn_kernel(q_ref, kc_ref, vc_ref, oc_ref, sb_ref, *, tq, nsp, ns, nc, pos0):
    q0 = pos0 + pl.program_id(1) * tq
    _cmp_attn_body(q_ref, [kc_ref[g] for g in range(N_KV)], [vc_ref[g] for g in range(N_KV)], oc_ref, sb_ref, q0,
                   tq=tq, nsp=nsp, ns=ns, nc=nc)


def _cmp_attn_body(q_ref, kcs, vcs, oc_ref, sb_ref, q0, *, tq, nsp, ns, nc):
    ncp = kcs[0].shape[0]
    rows = HPG * tq
    gw = HPG * HEAD_DIM
    nt = (((1,), (1,)), ((), ()))
    qs = [_stack_heads(q_ref[:, g * gw:(g + 1) * gw], tq).astype(BF) for g in range(N_KV)]
    ss = [lax.dot_general(qs[g], kcs[g], nt, preferred_element_type=F32) * SCALE for g in range(N_KV)]
    t = q0 + (lax.broadcasted_iota(jnp.int32, (rows, ncp), 0) & (tq - 1))
    n = lax.broadcasted_iota(jnp.int32, (rows, ncp), 1)
    mask = (n * CMP_STRIDE + (CMP_BLOCK - 1) <= t) & (n < nc)
    ps = []
    for g in range(N_KV):
        sm = jnp.where(mask, ss[g], NEG)
        m = jnp.max(sm, axis=-1, keepdims=True)
        e = jnp.where(mask, jnp.exp(sm - m), 0.0)
        l = jnp.sum(e, axis=-1, keepdims=True)
        ps.append(e / jnp.where(l > 0.0, l, 1.0))
    ocs = [jnp.dot(ps[g].astype(BF), vcs[g], preferred_element_type=F32) for g in range(N_KV)]
    for g in range(N_KV):
        for h in range(HPG):
            c0 = (g * HPG + h) * HEAD_DIM
            oc_ref[:, c0:c0 + HEAD_DIM] = ocs[g][h * tq:(h + 1) * tq]
    nn = lax.broadcasted_iota(jnp.int32, (ncp, nsp), 0)
    bb = lax.broadcasted_iota(jnp.int32, (ncp, nsp), 1)
    dd = nn - bb * (SEL_BLOCK // CMP_STRIDE)
    ov = ((dd >= 1 - CMP_BLOCK // CMP_STRIDE) & (dd < SEL_BLOCK // CMP_STRIDE) & (nn < nc)).astype(BF)
    bcol = lax.broadcasted_iota(jnp.int32, (tq, nsp), 1)
    bcolf = bcol.astype(F32)
    tcol = q0 + lax.broadcasted_iota(jnp.int32, (tq, nsp), 0)
    causal = bcol * SEL_BLOCK <= tcol
    forced = (bcol == (tcol >> SEL_SHIFT)) | (bcol == 0)
    scores = []
    for g in range(N_KV):
        p = ps[g]
        imp = p[0:tq] + p[tq:2 * tq] + p[2 * tq:3 * tq] + p[3 * tq:4 * tq]
        hi = imp.astype(BF)
        r1 = imp - hi.astype(F32)
        mid = r1.astype(BF)
        lo = (r1 - mid.astype(F32)).astype(BF)
        imp_sel = (jnp.dot(hi, ov, preferred_element_type=F32) + jnp.dot(mid, ov, preferred_element_type=F32)
                   + jnp.dot(lo, ov, preferred_element_type=F32))
        score = jnp.where(causal, imp_sel + jnp.where(forced, FORCE_BONUS, 0.0), NEG)
        scores.append(jnp.where(bcol < ns, score, -3e38))
    sels = [jnp.zeros((tq, nsp), jnp.bool_) for _ in range(N_KV)]
    for _ in range(min(N_SEL, ns)):
        mxs = [jnp.max(scores[g], axis=-1, keepdims=True) for g in range(N_KV)]
        idxs = [jnp.min(jnp.where(scores[g] == mxs[g], bcolf, float(nsp)), axis=-1, keepdims=True)
                for g in range(N_KV)]
        for g in range(N_KV):
            pick = bcolf == idxs[g]
            sels[g] = sels[g] | pick
            scores[g] = jnp.where(pick, -jnp.inf, scores[g])
    for g in range(N_KV):
        sb_ref[:, g * nsp:(g + 1) * nsp] = jnp.where(sels[g], 0.0, NEG)


def _cmp_attn(u5, kcvc, *, nseq, tq_total, tq, row0, t_kv, pos0):
    ncp = kcvc.shape[3]
    nc = (t_kv - CMP_BLOCK) // CMP_STRIDE + 1
    ns = -(-t_kv // SEL_BLOCK)
    nsp = -(-ns // LANES) * LANES
    nqt = tq_total // tq
    rb0 = row0 // tq
    hd = N_HEADS * HEAD_DIM
    kern = functools.partial(_cmp_attn_kernel, tq=tq, nsp=nsp, ns=ns, nc=nc, pos0=pos0)
    return pl.pallas_call(
        kern,
        grid=(nseq, nqt),
        in_specs=[pl.BlockSpec((tq, hd), lambda b, i: (rb0 + b * nqt + i, 0)),
                  pl.BlockSpec((None, None, N_KV, ncp, HEAD_DIM), lambda b, i: (b, 0, 0, 0, 0)),
                  pl.BlockSpec((None, None, N_KV, ncp, HEAD_DIM), lambda b, i: (b, 1, 0, 0, 0))],
        out_specs=[pl.BlockSpec((tq, hd), lambda b, i: (b * nqt + i, 0)),
                   pl.BlockSpec((tq, N_KV * nsp), lambda b, i: (b * nqt + i, 0))],
        out_shape=[jax.ShapeDtypeStruct((nseq * tq_total, hd), F32),
                   jax.ShapeDtypeStruct((nseq * tq_total, N_KV * nsp), F32)],
        compiler_params=_cparams(("parallel", "arbitrary")),
        name="cmp_attn",
    )(u5, kcvc, kcvc)


EXP2_SCALE = SCALE * 1.4426950408889634


def _sel_win_kernel(q_ref, sb_ref, gt_ref, oc_ref, ks_ref, vs_ref, kw_ref, vw_ref, o_ref,
                    m_ref, acc_ref, s_ref, *, tq, nsp, tk):
    g = pl.program_id(1)
    q0 = pl.program_id(2) * tq
    rows = HPG * tq
    nt = (((1,), (1,)), ((), ()))
    q = _stack_heads(q_ref[...], tq).astype(BF)
    bias = jnp.concatenate([sb_ref[...]] * HPG, axis=0).astype(BF)
    qa = jnp.concatenate([q, bias], axis=1)
    ones = jnp.ones((tk, HEAD_DIM), BF)

    m_ref[...] = jnp.full(m_ref.shape, -jnp.inf, F32)
    acc_ref[...] = jnp.zeros(acc_ref.shape, F32)

    def scores(j):
        k0 = pl.multiple_of(j * tk, tk)
        blk = (k0 + lax.broadcasted_iota(jnp.int32, (tk, nsp), 0)) >> SEL_SHIFT
        onehot = (blk == lax.broadcasted_iota(jnp.int32, (tk, nsp), 1)).astype(BF)
        ka = jnp.concatenate([ks_ref[pl.ds(k0, tk), :], onehot], axis=1)
        return [lax.dot_general(qa[h * tq:(h + 1) * tq], ka, nt, preferred_element_type=F32) for h in range(HPG)]

    def consume(ss, j, masked):
        k0 = pl.multiple_of(j * tk, tk)
        va = jnp.concatenate([vs_ref[pl.ds(k0, tk), :], ones], axis=1)
        if masked:
            ok = (k0 + lax.broadcasted_iota(jnp.int32, (tq, tk), 1)
                  <= q0 + lax.broadcasted_iota(jnp.int32, (tq, tk), 0))
            ss = [jnp.where(ok, sh, NEG) for sh in ss]
        ps, alphas = [], []
        for h in range(HPG):
            rs = slice(h * tq, (h + 1) * tq)
            m_old = m_ref[rs]
            m_new = jnp.maximum(m_old, jnp.max(ss[h], axis=-1, keepdims=True))
            alphas.append(jnp.exp2((m_old - m_new) * EXP2_SCALE))
            ps.append(jnp.exp2((ss[h] - m_new) * EXP2_SCALE).astype(BF))
            m_ref[rs] = m_new
        for h in range(HPG):
            rs = slice(h * tq, (h + 1) * tq)
            acc_ref[rs] = alphas[h] * acc_ref[rs] + jnp.dot(ps[h], va, preferred_element_type=F32)

    j_last = (q0 + tq - 1) // tk
    for h, sh in enumerate(scores(0)):
        s_ref[h] = sh

    def sel_body(j, carry):
        cur = [s_ref[h] for h in range(HPG)]
        nxt = scores(j + 1)
        consume(cur, j, False)
        for h in range(HPG):
            s_ref[h] = nxt[h]
        return carry

    lax.fori_loop(0, j_last, sel_body, 0)
    nkw = WINDOW + tq
    w0 = pl.multiple_of(jnp.maximum(q0 - WINDOW, 0), tq)
    sw = lax.dot_general(q, kw_ref[pl.ds(w0, nkw), :], nt, preferred_element_type=F32)
    consume([s_ref[h] for h in range(HPG)], j_last, True)
    acc = acc_ref[...]
    o_s = acc[:, :HEAD_DIM] / acc[:, HEAD_DIM:HEAD_DIM + 1]
    vwa = jnp.concatenate([vw_ref[pl.ds(w0, nkw), :], jnp.ones((nkw, HEAD_DIM), BF)], axis=1)
    t = q0 + (lax.broadcasted_iota(jnp.int32, (rows, nkw), 0) & (tq - 1))
    kpos = w0 + lax.broadcasted_iota(jnp.int32, (rows, nkw), 1)
    sw = jnp.where((kpos <= t) & (kpos > t - WINDOW), sw, NEG)
    pw = jnp.exp2((sw - jnp.max(sw, axis=-1, keepdims=True)) * EXP2_SCALE)
    aw = jnp.dot(pw.astype(BF), vwa, preferred_element_type=F32)
    o_w = aw[:, :HEAD_DIM] / aw[:, HEAD_DIM:HEAD_DIM + 1]

    gts = gt_ref[...]
    oc = oc_ref[...]
    lane = lax.broadcasted_iota(jnp.int32, gts.shape, 1)
    for h in range(HPG):
        base = (g * HPG + h) * 3
        gc = jnp.sum(jnp.where(lane == base, gts, 0.0), axis=-1, keepdims=True)
        gs = jnp.sum(jnp.where(lane == base + 1, gts, 0.0), axis=-1, keepdims=True)
        gw = jnp.sum(jnp.where(lane == base + 2, gts, 0.0), axis=-1, keepdims=True)
        sl = slice(h * HEAD_DIM, (h + 1) * HEAD_DIM)
        rs = slice(h * tq, (h + 1) * tq)
        o_ref[:, sl] = gc * oc[:, sl] + gs * o_s[rs] + gw * o_w[rs]


def _sel_win_attn(u5, sbias, gates, oc, ksvs, kwvw, *, tq, tk):
    t = ksvs.shape[0]
    nsp = sbias.shape[1] // N_KV
    nqt = t // tq
    gw = HPG * HEAD_DIM
    assert t % tk == 0 and t % tq == 0 and t >= WINDOW + tq and WINDOW % tq == 0
    kern = functools.partial(_sel_win_kernel, tq=tq, nsp=nsp, tk=tk)
    return pl.pallas_call(
        kern,
        grid=(1, N_KV, nqt),
        in_specs=[pl.BlockSpec((tq, gw), lambda b, g, i: (i, g)),
                  pl.BlockSpec((tq, nsp), lambda b, g, i: (i, g)),
                  pl.BlockSpec((tq, LANES), lambda b, g, i: (i, 0)),
                  pl.BlockSpec((tq, gw), lambda b, g, i: (i, g)),
                  pl.BlockSpec((t, HEAD_DIM), lambda b, g, i: (0, g)),
                  pl.BlockSpec((t, HEAD_DIM), lambda b, g, i: (0, N_KV + g)),
                  pl.BlockSpec((t, HEAD_DIM), lambda b, g, i: (0, g)),
                  pl.BlockSpec((t, HEAD_DIM), lambda b, g, i: (0, N_KV + g))],
        out_specs=pl.BlockSpec((tq, gw), lambda b, g, i: (i, g)),
        out_shape=jax.ShapeDtypeStruct((t, N_KV * gw), F32),
        scratch_shapes=[pltpu.VMEM((HPG * tq, 1), F32), pltpu.VMEM((HPG * tq, 2 * HEAD_DIM), F32),
                        pltpu.VMEM((HPG, tq, tk), F32)],
        compiler_params=_cparams(("parallel", "arbitrary", "arbitrary")),
        name="sel_win_prompt",
    )(u5, sbias, gates, oc, ksvs, ksvs, kwvw, kwvw)


ROWS_PER_TOKEN = 4 * N_KV
CMP_PAGES_PER_STEP = 8


def _compress_paged_kernel(pt_ref, *refs, pg, page_size):
    page_refs = refs[:pg]
    w1_ref, o_ref, a_ref = refs[pg:]
    cpp = page_size // CMP_STRIDE
    for k in range(pg):
        xc = pltpu.einshape("tcd->ctd", page_refs[k][...])
        for wg in range(2 * N_KV):
            w, g = divmod(wg, N_KV)
            r0 = (g * pg + k) * cpp
            ys = pltpu.einshape("nsd->snd", xc[wg].reshape(cpp, CMP_STRIDE, HEAD_DIM))
            for s in range(CMP_STRIDE):
                a_ref[w, r0:r0 + cpp, s * HEAD_DIM:(s + 1) * HEAD_DIM] = ys[s]
    for w in range(2):
        p = jnp.dot(a_ref[w].astype(BF), w1_ref[w], preferred_element_type=F32)
        for g in range(N_KV):
            o_ref[w, g] = p[g * pg * cpp:(g + 1) * pg * cpp]


def _compress_paged(page_table, cache_tok, w1cat, *, page_size):
    bs, n_pages = page_table.shape
    pg = CMP_PAGES_PER_STEP
    assert n_pages % pg == 0
    cpp = page_size // CMP_STRIDE
    kd = CMP_STRIDE * HEAD_DIM
    page_spec = lambda k: pl.BlockSpec((page_size, 2 * N_KV, HEAD_DIM), lambda b, j, pt: (pt[b, j * pg + k], 0, 0))
    grid_spec = pltpu.PrefetchScalarGridSpec(
        num_scalar_prefetch=1,
        grid=(bs, n_pages // pg),
        in_specs=[page_spec(k) for k in range(pg)] + [pl.BlockSpec((2, kd, 2 * HEAD_DIM), lambda b, j, pt: (0, 0, 0))],
        out_specs=pl.BlockSpec((None, 2, N_KV, pg * cpp, 2 * HEAD_DIM), lambda b, j, pt: (b, 0, 0, j, 0)),
        scratch_shapes=[pltpu.VMEM((2, N_KV * pg * cpp, kd), F32)],
    )
    return pl.pallas_call(
        functools.partial(_compress_paged_kernel, pg=pg, page_size=page_size),
        grid_spec=grid_spec,
        out_shape=jax.ShapeDtypeStruct((bs, 2, N_KV, n_pages * cpp, 2 * HEAD_DIM), F32),
        compiler_params=_cparams(("parallel", "arbitrary")),
        name="compress_paged",
    )(page_table, *([cache_tok] * pg), w1cat)


def _compress_finish(p, pe_term, w2):
    nch = p.shape[0]
    pre = p[:, :HEAD_DIM] + pltpu.roll(p[:, HEAD_DIM:], nch - 1, 0) + pe_term
    return jnp.dot(jax.nn.gelu(pre).astype(BF), w2, preferred_element_type=F32).astype(BF)


def _cmp_attn_paged_kernel(q_ref, pk_ref, pv_ref, pe_ref, w1f_ref, w2_ref, oc_ref, sb_ref, *, tq, nsp, ns, nc, pos0):
    pe_k = jnp.dot(pe_ref[0].astype(BF), w1f_ref[0], preferred_element_type=F32)[0:1, :]
    pe_v = jnp.dot(pe_ref[1].astype(BF), w1f_ref[1], preferred_element_type=F32)[0:1, :]
    kcs = [_compress_finish(pk_ref[g], pe_k, w2_ref[0]) for g in range(N_KV)]
    vcs = [_compress_finish(pv_ref[g], pe_v, w2_ref[1]) for g in range(N_KV)]
    _cmp_attn_body(q_ref, kcs, vcs, oc_ref, sb_ref, pos0, tq=tq, nsp=nsp, ns=ns, nc=nc)


def _cmp_attn_paged(u5, pcmp, pe8, w1flat, w2, *, tq, row0, t_kv, pos0):
    bs, _, _, nch, _ = pcmp.shape
    nc = (t_kv - CMP_BLOCK) // CMP_STRIDE + 1
    ns = -(-t_kv // SEL_BLOCK)
    nsp = -(-ns // LANES) * LANES
    rb0 = row0 // tq
    hd = N_HEADS * HEAD_DIM
    kd = CMP_STRIDE * HEAD_DIM
    kern = functools.partial(_cmp_attn_paged_kernel, tq=tq, nsp=nsp, ns=ns, nc=nc, pos0=pos0)
    return pl.pallas_call(
        kern,
        grid=(bs,),
        in_specs=[pl.BlockSpec((tq, hd), lambda b: (rb0 + b, 0)),
                  pl.BlockSpec((None, None, N_KV, nch, 2 * HEAD_DIM), lambda b: (b, 0, 0, 0, 0)),
                  pl.BlockSpec((None, None, N_KV, nch, 2 * HEAD_DIM), lambda b: (b, 1, 0, 0, 0)),
                  pl.BlockSpec((2, SUBLANES, 2 * kd), lambda b: (0, 0, 0)),
                  pl.BlockSpec((2, 2 * kd, HEAD_DIM), lambda b: (0, 0, 0)),
                  pl.BlockSpec((2, HEAD_DIM, HEAD_DIM), lambda b: (0, 0, 0))],
        out_specs=[pl.BlockSpec((tq, hd), lambda b: (b, 0)),
                   pl.BlockSpec((tq, N_KV * nsp), lambda b: (b, 0))],
        out_shape=[jax.ShapeDtypeStruct((bs * tq, hd), F32),
                   jax.ShapeDtypeStruct((bs * tq, N_KV * nsp), F32)],
        compiler_params=_cparams(("parallel",)),
        name="cmp_attn_paged",
    )(u5, pcmp, pcmp, pe8, w1flat, w2)


SEL_PAGES_PER_STEP = 8


def _sel_win_paged_kernel(pt_ref, *refs, pps, n_pages, page_size, nsp, tq, win_keep, past_len):
    page_refs = refs[:pps]
    wq_ref, newkv_ref, wcache_ref, newwin_ref, gt_ref, oc_ref, o_ref, m_ref, l_ref, acc_ref = refs[pps:]
    j = pl.program_id(1)
    n_steps = n_pages // pps
    rows = N_HEADS * tq
    kvd = N_KV * HEAD_DIM
    wq = wq_ref[...]
    nt = (((1,), (1,)), ((), ()))

    @pl.when(j == 0)
    def _():
        m_ref[...] = jnp.full(m_ref.shape, -jnp.inf, F32)
        l_ref[...] = jnp.zeros(l_ref.shape, F32)
        acc_ref[...] = jnp.zeros(acc_ref.shape, F32)

    def tile(k_all, v_all, first_block, mask):
        nk = k_all.shape[0]
        blk = first_block + (lax.broadcasted_iota(jnp.int32, (nk, nsp), 0) >> SEL_SHIFT)
        onehot = (blk == lax.broadcasted_iota(jnp.int32, (nk, nsp), 1)).astype(BF)
        s = lax.dot_general(wq, jnp.concatenate([k_all, onehot], axis=1), nt, preferred_element_type=F32)
        if mask is not None:
            s = jnp.where(mask, s, NEG)
        m_old = m_ref[...]
        m_new = jnp.maximum(m_old, jnp.max(s, axis=-1, keepdims=True))
        alpha = jnp.exp2((m_old - m_new) * EXP2_SCALE)
        p = jnp.exp2((s - m_new) * EXP2_SCALE)
        l_ref[...] = alpha * l_ref[...] + jnp.sum(p, axis=-1, keepdims=True)
        acc_ref[...] = alpha * acc_ref[...] + jnp.dot(p.astype(BF), v_all, preferred_element_type=F32)
        m_ref[...] = m_new

    def page_rows(which):
        tr = [pltpu.einshape("tcd->ctd", ref[...]) for ref in page_refs]
        return jnp.concatenate(
            [jnp.concatenate([t[which * N_KV + g] for g in range(N_KV)], axis=1) for t in tr],
            axis=0).astype(BF)

    @pl.when(j < n_steps)
    def _():
        tile(page_rows(0), page_rows(1), j * (pps * page_size // SEL_BLOCK), None)

    @pl.when(j == n_steps)
    def _():
        qi = lax.broadcasted_iota(jnp.int32, (rows, page_size), 0) & (tq - 1)
        key = lax.broadcasted_iota(jnp.int32, (rows, page_size), 1)
        nk = newkv_ref[...]
        zpad = jnp.zeros((page_size - tq, kvd), F32)
        k_all = jnp.concatenate([nk[:, :kvd], zpad], axis=0).astype(BF)
        v_all = jnp.concatenate([nk[:, kvd:], zpad], axis=0).astype(BF)
        tile(k_all, v_all, n_pages * (page_size // SEL_BLOCK), key <= qi)
        o_s = acc_ref[...] / l_ref[...]

        nw = newwin_ref[...]
        kc = jnp.concatenate([wcache_ref[pl.ds(g, win_keep, stride=2 * N_KV), :] for g in range(N_KV)], axis=1)
        vc = jnp.concatenate([wcache_ref[pl.ds(N_KV + g, win_keep, stride=2 * N_KV), :] for g in range(N_KV)], axis=1)
        kw = jnp.concatenate([kc, nw[:, :kvd], zpad], axis=0).astype(BF)
        vw = jnp.concatenate([vc, nw[:, kvd:], zpad], axis=0).astype(BF)
        nkw = win_keep + page_size
        sw = lax.dot_general(wq[:, :kvd], kw, nt, preferred_element_type=F32)
        t = past_len + (lax.broadcasted_iota(jnp.int32, (rows, nkw), 0) & (tq - 1))
        kpos = (past_len - win_keep) + lax.broadcasted_iota(jnp.int32, (rows, nkw), 1)
        sw = jnp.where((kpos <= t) & (kpos > t - WINDOW), sw, NEG)
        pw = jnp.exp2((sw - jnp.max(sw, axis=-1, keepdims=True)) * EXP2_SCALE)
        o_w = jnp.dot(pw.astype(BF), vw, preferred_element_type=F32) / jnp.sum(pw, axis=-1, keepdims=True)

        gts = gt_ref[...]
        oc = oc_ref[...]
        lane = lax.broadcasted_iota(jnp.int32, gts.shape, 1)
        for hh in range(N_HEADS):
            g = hh // HPG
            gc = jnp.sum(jnp.where(lane == hh * 3, gts, 0.0), axis=-1, keepdims=True)
            gs = jnp.sum(jnp.where(lane == hh * 3 + 1, gts, 0.0), axis=-1, keepdims=True)
            gw = jnp.sum(jnp.where(lane == hh * 3 + 2, gts, 0.0), axis=-1, keepdims=True)
            rs = slice(hh * tq, (hh + 1) * tq)
            cs = slice(g * HEAD_DIM, (g + 1) * HEAD_DIM)
            os_ = slice(hh * HEAD_DIM, (hh + 1) * HEAD_DIM)
            o_ref[:, os_] = gc * oc[:, os_] + gs * o_s[rs, cs] + gw * o_w[rs, cs]


def _sel_win_paged(page_table, cache_tok, wq, u5, wcache_flat, gates, oc, *, tq, row0, page_size, win_keep, past_len):
    bs, n_pages = page_table.shape
    pps = SEL_PAGES_PER_STEP
    assert n_pages % pps == 0
    n_steps = n_pages // pps
    rows = N_HEADS * tq
    kvd = N_KV * HEAD_DIM
    nsp = wq.shape[2] - kvd
    rb0 = row0 // tq
    sel_col = (N_HEADS * HEAD_DIM + 2 * kvd) // (2 * kvd)
    kern = functools.partial(_sel_win_paged_kernel, pps=pps, n_pages=n_pages, page_size=page_size, nsp=nsp, tq=tq,
                             win_keep=win_keep, past_len=past_len)
    page_spec = lambda k: pl.BlockSpec(
        (page_size, 2 * N_KV, HEAD_DIM), lambda b, j, pt: (pt[b, jnp.minimum(j, n_steps - 1) * pps + k], 1, 0))
    grid_spec = pltpu.PrefetchScalarGridSpec(
        num_scalar_prefetch=1,
        grid=(bs, n_steps + 1),
        in_specs=[page_spec(k) for k in range(pps)] + [
            pl.BlockSpec((None, rows, kvd + nsp), lambda b, j, pt: (b, 0, 0)),
            pl.BlockSpec((tq, 2 * kvd), lambda b, j, pt: (rb0 + b, sel_col)),
            pl.BlockSpec((win_keep * 2 * N_KV, HEAD_DIM), lambda b, j, pt: (b, 0)),
            pl.BlockSpec((tq, 2 * kvd), lambda b, j, pt: (rb0 + b, sel_col + 1)),
            pl.BlockSpec((tq, LANES), lambda b, j, pt: (rb0 + b, 0)),
            pl.BlockSpec((tq, N_HEADS * HEAD_DIM), lambda b, j, pt: (b, 0))],
        out_specs=pl.BlockSpec((tq, N_HEADS * HEAD_DIM), lambda b, j, pt: (b, 0)),
        scratch_shapes=[pltpu.VMEM((rows, 1), F32), pltpu.VMEM((rows, 1), F32), pltpu.VMEM((rows, kvd), F32)],
    )
    return pl.pallas_call(
        kern,
        grid_spec=grid_spec,
        out_shape=jax.ShapeDtypeStruct((bs * tq, N_HEADS * HEAD_DIM), F32),
        compiler_params=_cparams(("parallel", "arbitrary")),
        name="sel_win_paged",
    )(page_table, *([cache_tok] * pps), wq, u5, wcache_flat, u5, gates, oc)


def _gelu(u):
    return jax.nn.gelu(u)


def _identity(u):
    return u


def kernel(x_prompt, x_sample, state_conv, state_h, cache_kv, cache_win, page_table, p_prompt, p_sample,
           lru_w_in, lru_conv_w, lru_conv_b, lru_w_a, lru_b_a, lru_w_x, lru_b_x, lru_lambda, lru_w_out,
           nsa_w_in, nsa_cmp_w1, nsa_cmp_pe, nsa_cmp_w2, nsa_w_out,
           ln1_g, ln1_b, ln2_g, ln2_b,
           moe_w_group, moe_b_group, moe_w_expert, moe_b_expert, moe_w_gate, moe_w_up, moe_w_down,
           ple_w_gate, ple_w_proj):
    bp, tp, d = x_prompt.shape
    bs, ts, _ = x_sample.shape
    assert bp == 1 and ts == SUBLANES
    n_p = bp * tp
    n_s = bs * ts
    n = n_p + n_s
    past_len = page_table.shape[1] * cache_kv.shape[2]
    d_rnn = lru_w_out.shape[1]
    hd = N_HEADS * HEAD_DIM
    kvw = 2 * N_KV * HEAD_DIM

    x0 = jnp.concatenate([x_prompt.reshape(n_p, d), x_sample.reshape(n_s, d)], axis=0)

    def moe_and_ple(x1, layer):
        w_route = jnp.zeros((d, LANES), F32).at[:, :N_EXPERTS].set(moe_w_expert[layer])
        w_route = w_route.at[:, N_EXPERTS:N_EXPERTS + N_GROUPS].set(moe_w_group[layer])
        b_route = jnp.zeros((1, LANES), F32).at[0, :N_EXPERTS].set(moe_b_expert[layer])
        b_route = b_route.at[0, N_EXPERTS:N_EXPERTS + N_GROUPS].set(moe_b_group[layer])
        y0, y1 = _moe_layer(x1, w_route, b_route, moe_w_gate, moe_w_up, moe_w_down, layer)
        p_cat = jnp.concatenate([p_prompt[layer].reshape(n_p, -1), p_sample[layer].reshape(n_s, -1)], axis=0)
        return _ln_ple(x1, y0, y1, ln2_g[layer].reshape(1, d), ln2_b[layer].reshape(1, d),
                       ple_w_gate[layer].astype(BF), p_cat, ple_w_proj[layer].astype(BF))

    w_in = lru_w_in[0].astype(BF)
    x0b = x0.astype(BF)
    gate = _mm_act(x0b, w_in[:, :d_rnn], _gelu)
    xr = _mm_act(x0b, w_in[:, d_rnn:], _identity)
    lru_args = (lru_conv_w[0], lru_conv_b[0].reshape(1, d_rnn), lru_w_a[0].astype(BF), lru_b_a[0].reshape(1, d_rnn),
                lru_w_x[0].astype(BF), lru_b_x[0].reshape(1, d_rnn), lru_lambda[0].reshape(1, d_rnn))
    hg_p, hl_p = _rglru_core(xr, gate, jnp.zeros((1, SUBLANES, d_rnn), F32), jnp.zeros((1, 1, d_rnn), F32),
                             *lru_args, chain=True, tt=_row_tile(n_p, 512), row0=0, n=n_p)
    prev_s = jnp.pad(state_conv[0], ((0, 0), (SUBLANES - (CONV_W - 1), 0), (0, 0)))
    hg_s, hl_s = _rglru_core(xr, gate, prev_s, state_h[0].reshape(bs, 1, d_rnn),
                             *lru_args, chain=False, tt=n_s, row0=n_p, n=n_s)
    hg = jnp.concatenate([hg_p, hg_s], axis=0)
    conv_prompt = xr[n_p - (CONV_W - 1):n_p].reshape(1, bp, CONV_W - 1, d_rnn)
    h_prompt = hl_p[n_p // SUBLANES - 1].reshape(1, bp, d_rnn)
    conv_sample = xr[n_p:].reshape(bs, ts, d_rnn)[:, ts - (CONV_W - 1):].reshape(1, bs, CONV_W - 1, d_rnn)
    h_sample = hl_s.reshape(1, bs, d_rnn)
    x1 = _mm_res_ln(hg, lru_w_out[0].astype(BF), x0, ln1_g[0].reshape(1, d), ln1_b[0].reshape(1, d))
    xa = moe_and_ple(x1, 0)

    w_nsa = nsa_w_in[0]
    n_main = hd + 3 * kvw
    pos = jnp.concatenate([jnp.arange(n_p, dtype=jnp.int32) % tp,
                           past_len + jnp.arange(n_s, dtype=jnp.int32) % ts])
    half = HEAD_DIM // 2
    inv = ROPE_THETA ** (-jnp.arange(half, dtype=F32) / half)
    ang = pos.astype(F32)[:, None] * inv[None, :]
    cos = jnp.concatenate([jnp.cos(ang), jnp.cos(ang)], axis=1)
    sin_signed = jnp.concatenate([-jnp.sin(ang), jnp.sin(ang)], axis=1)
    xab = xa.astype(BF)
    u5, ukv = _nsa_in(xab, w_nsa[:, :n_main].astype(BF), cos, sin_signed)
    w_gates = jnp.zeros((d, LANES), F32).at[:, :3 * N_HEADS].set(w_nsa[:, n_main:]).astype(BF)
    gates = _mm_act(xab, w_gates, jax.nn.sigmoid)

    kv_prompt = u5[:n_p, hd:hd + 2 * kvw].reshape(1, bp, tp, 4, N_KV, HEAD_DIM)
    kv_sample = u5[n_p:, hd:hd + 2 * kvw].reshape(1, bs, ts, 4, N_KV, HEAD_DIM)
    win_new_p = u5[:n_p, hd + 2 * kvw:]
    win_new_s = u5[n_p:, hd + 2 * kvw:].reshape(bs, ts, kvw)
    win_keep_p = min(WINDOW, tp)
    win_prompt = win_new_p[n_p - win_keep_p:].reshape(1, bp, win_keep_p, 2, N_KV, HEAD_DIM)
    win_keep_s = cache_win.shape[2]
    win_sample = jnp.concatenate([cache_win[0][:, ts:], win_new_s.reshape(bs, ts, 2, N_KV, HEAD_DIM)], axis=1)[None]

    kd = CMP_STRIDE * HEAD_DIM
    w1 = nsa_cmp_w1[0]
    w1flat = w1.reshape(2, 2 * kd, HEAD_DIM).astype(BF)
    w1cat = jnp.concatenate([w1flat[:, :kd], w1flat[:, kd:]], axis=2)
    pe8 = jnp.zeros((2, SUBLANES, 2 * kd), F32).at[:, 0].set(nsa_cmp_pe[0].reshape(2, 2 * kd))
    w2 = nsa_cmp_w2[0].astype(BF)

    kcvc_p = _compress(u5[:n_p, hd:hd + kvw].reshape(bp, tp, kvw), w1cat, pe8, w1flat, w2)
    tq_p = _row_tile(tp, 128)
    oc_p, sb_p = _cmp_attn(u5, kcvc_p, nseq=bp, tq_total=tp, tq=tq_p, row0=0, t_kv=tp, pos0=0)
    ksvs_p = ukv[:n_p, :kvw]
    kwvw_p = ukv[:n_p, kvw:]
    o_p = _sel_win_attn(u5, sb_p, gates, oc_p, ksvs_p, kwvw_p, tq=_row_tile(tp, 256), tk=_row_tile(tp, 1024))

    page_size = cache_kv.shape[2]
    cache_tok = cache_kv[0].reshape(-1, ROWS_PER_TOKEN, HEAD_DIM)
    pcmp = _compress_paged(page_table, cache_tok, w1cat, page_size=page_size)
    t_kv_s = past_len + ts
    oc_s, sb_s = _cmp_attn_paged(u5, pcmp, pe8, w1flat, w2, tq=ts, row0=n_p, t_kv=t_kv_s, pos0=past_len)
    nsp_s = sb_s.shape[1] // N_KV
    q5 = u5[n_p:, :hd].reshape(bs, ts, N_KV, HPG, HEAD_DIM)
    eye = jnp.eye(N_KV, dtype=F32)[None, None, :, None, :, None]
    q_diag = (q5[:, :, :, :, None, :] * eye).transpose(0, 2, 3, 1, 4, 5).reshape(bs, N_HEADS * ts, N_KV * HEAD_DIM)
    bias = jnp.broadcast_to(sb_s.reshape(bs, ts, N_KV, 1, nsp_s).transpose(0, 2, 3, 1, 4),
                            (bs, N_KV, HPG, ts, nsp_s)).reshape(bs, N_HEADS * ts, nsp_s)
    wq = jnp.concatenate([q_diag, bias], axis=2).astype(BF)
    o_s = _sel_win_paged(page_table, cache_tok, wq, u5, cache_win[0].reshape(-1, HEAD_DIM), gates, oc_s,
                         tq=ts, row0=n_p, page_size=page_size, win_keep=win_keep_s, past_len=past_len)

    o_all = jnp.concatenate([o_p, o_s], axis=0)
    x1 = _mm_res_ln(o_all, nsa_w_out[0].astype(BF), xa, ln1_g[1].reshape(1, d), ln1_b[1].reshape(1, d))
    xo = moe_and_ple(x1, 1)

    y_prompt = xo[:n_p].reshape(bp, tp, d)
    y_sample = xo[n_p:].reshape(bs, ts, d)
    return (y_prompt, y_sample, conv_prompt, h_prompt, kv_prompt, win_prompt,
            conv_sample, h_sample, kv_sample, win_sample)
```

```python
import functools

import jax
import jax.numpy as jnp
from jax import lax
from jax.experimental import pallas as pl
from jax.experimental.pallas import tpu as pltpu

F32 = jnp.float32
BF = jnp.bfloat16

HEAD_DIM = 128
N_HEADS = 16
N_KV = 4
HPG = N_HEADS // N_KV
LRU_BLOCK = 256
CONV_W = 4
LRU_C = 8.0
CMP_BLOCK = 32
CMP_STRIDE = 16
SEL_BLOCK = 64
N_SEL = 16
WINDOW = 512
ROPE_THETA = 10000.0
N_GROUPS = 4
EPG = 8
N_EXPERTS = N_GROUPS * EPG
DEPTH = 2
ALPHA = (2 * DEPTH) ** 0.25
LN_EPS = 1e-5
NEG = -1e30
FORCE_BONUS = 1e4
SCALE = HEAD_DIM ** -0.5
SEL_SHIFT = SEL_BLOCK.bit_length() - 1
EPG_SHIFT = EPG.bit_length() - 1

LANES = 128
SUBLANES = 8
VMEM_LIMIT = 56 * 1024 * 1024
MOE_TM = 1024
MOE_SUB = 256
MOE_TF = 256


def _cparams(sem):
    return pltpu.CompilerParams(dimension_semantics=sem, vmem_limit_bytes=VMEM_LIMIT)


def _row_tile(n, cap):
    best = None
    for t in range(SUBLANES, cap + 1, SUBLANES):
        if n % t == 0:
            best = t
    assert best is not None, (n, cap)
    return best


def _mm_act_kernel(x_ref, w_ref, o_ref, *, act):
    u = jnp.dot(x_ref[...].astype(BF), w_ref[...], preferred_element_type=F32)
    o_ref[...] = act(u).astype(o_ref.dtype)


def _mm_act(x, w, act, out_dtype=F32, tn=512):
    m, k = x.shape
    n = w.shape[1]
    tm = _row_tile(m, 768)
    tn = min(tn, n)
    assert n % tn == 0
    return pl.pallas_call(
        functools.partial(_mm_act_kernel, act=act),
        grid=(m // tm, n // tn),
        in_specs=[pl.BlockSpec((tm, k), lambda i, j: (i, 0)),
                  pl.BlockSpec((k, tn), lambda i, j: (0, j))],
        out_specs=pl.BlockSpec((tm, tn), lambda i, j: (i, j)),
        out_shape=jax.ShapeDtypeStruct((m, n), out_dtype),
        compiler_params=_cparams(("parallel", "arbitrary")),
        name="mm_act",
    )(x, w)


def _rglru_kernel(xr_ref, prev_ref, h0_ref, gate_ref, cw_ref, cb_ref, wa_ref, ba_ref, wx_ref, bx_ref,
                  lam_ref, hg_ref, hlast_ref, carry_prev, carry_h, h_scr, *, chain):
    tt, c = xr_ref.shape
    g = tt // SUBLANES
    x = xr_ref[...].reshape(g, SUBLANES, c)
    if chain:
        first = pl.program_id(1) == 0
        p0 = jnp.where(first, prev_ref[...], carry_prev[...])
        prev = jnp.concatenate([p0, x[:-1]], axis=0) if g > 1 else p0
        h_start = jnp.where(first, h0_ref[...], carry_h[...])
    else:
        prev = prev_ref[...]
    sub = lax.broadcasted_iota(jnp.int32, (g, SUBLANES, c), 1)
    cw = [cw_ref[k:k + 1, :].reshape(1, 1, c) for k in range(CONV_W)]
    xc = cb_ref[...].reshape(1, 1, c) + x * cw[CONV_W - 1]
    for s in range(1, CONV_W):
        sh = jnp.where(sub < s, pltpu.roll(prev, s, 1), pltpu.roll(x, s, 1))
        xc = xc + sh * cw[CONV_W - 1 - s]
    xc2 = xc.reshape(tt, c)
    xb = xc2.astype(BF)
    r = jax.nn.sigmoid(jnp.dot(xb, wa_ref[...], preferred_element_type=F32) + ba_ref[...])
    i = jax.nn.sigmoid(jnp.dot(xb, wx_ref[...], preferred_element_type=F32) + bx_ref[...])
    log_a = (-LRU_C * r) * jax.nn.softplus(-lam_ref[...])
    a = jnp.exp(log_a)
    bterm = jnp.sqrt(-jnp.tanh(log_a) * (a * a + 1.0)) * (i * xc2)
    a3 = a.reshape(g, SUBLANES, c)
    b3 = bterm.reshape(g, SUBLANES, c)
    for d in (1, 2, 4):
        a_sh = jnp.where(sub < d, 1.0, pltpu.roll(a3, d, 1))
        b_sh = jnp.where(sub < d, 0.0, pltpu.roll(b3, d, 1))
        b3 = a3 * b_sh + b3
        a3 = a3 * a_sh
    if chain:
        hs = []
        hprev = h_start.reshape(1, c)
        for j in range(g):
            hj = a3[j] * hprev + b3[j]
            hs.append(hj)
            hprev = hj[SUBLANES - 1:SUBLANES, :]
        h3 = jnp.stack(hs, axis=0)
        carry_prev[...] = x[g - 1:g]
        carry_h[...] = hprev.reshape(1, 1, c)
    else:
        h3 = a3 * h0_ref[...] + b3
    h2 = h3.reshape(tt, c)
    hg_ref[...] = (h2 * gate_ref[...]).astype(hg_ref.dtype)
    for k in range(c // LANES):
        h_scr[k] = h2[:, k * LANES:(k + 1) * LANES]
        hlast_ref[:, k * LANES:(k + 1) * LANES] = h_scr[k, pl.ds(SUBLANES - 1, g, stride=SUBLANES), :]


def _rglru_core(xr, gate, prev, h0, cw, cb, wa, ba, wx, bx, lam, *, chain, tt, row0, n):
    d = xr.shape[1]
    rb0 = row0 // tt
    assert row0 % tt == 0 and n % tt == 0
    c = LRU_BLOCK
    g = tt // SUBLANES
    pg = 1 if chain else g
    vec = lambda: pl.BlockSpec((1, c), lambda nb, t: (0, nb))
    return pl.pallas_call(
        functools.partial(_rglru_kernel, chain=chain),
        grid=(d // c, n // tt),
        in_specs=[pl.BlockSpec((tt, c), lambda nb, t: (rb0 + t, nb)),
                  pl.BlockSpec((pg, SUBLANES, c), lambda nb, t: (0 if chain else t, 0, nb)),
                  pl.BlockSpec((pg, 1, c), lambda nb, t: (0 if chain else t, 0, nb)),
                  pl.BlockSpec((tt, c), lambda nb, t: (rb0 + t, nb)),
                  pl.BlockSpec((CONV_W, c), lambda nb, t: (0, nb)),
                  vec(),
                  pl.BlockSpec((None, c, c), lambda nb, t: (nb, 0, 0)),
                  vec(),
                  pl.BlockSpec((None, c, c), lambda nb, t: (nb, 0, 0)),
                  vec(), vec()],
        out_specs=[pl.BlockSpec((tt, c), lambda nb, t: (t, nb)),
                   pl.BlockSpec((g, c), lambda nb, t: (t, nb))],
        out_shape=[jax.ShapeDtypeStruct((n, d), BF), jax.ShapeDtypeStruct((n // SUBLANES, d), F32)],
        scratch_shapes=[pltpu.VMEM((1, SUBLANES, c), F32), pltpu.VMEM((1, 1, c), F32),
                        pltpu.VMEM((c // LANES, tt, LANES), F32)],
        compiler_params=_cparams(("parallel", "arbitrary")),
        name="rglru_core",
    )(xr, prev, h0, gate, cw, cb, wa, ba, wx, bx, lam)


def _layer_norm(z, g, b):
    mu = jnp.mean(z, axis=-1, keepdims=True)
    zc = z - mu
    var = jnp.mean(zc * zc, axis=-1, keepdims=True)
    return zc * lax.rsqrt(var + LN_EPS) * g + b


def _mm_res_ln_kernel(a_ref, w_ref, res_ref, g_ref, b_ref, o_ref):
    y = jnp.dot(a_ref[...].astype(BF), w_ref[...], preferred_element_type=F32)
    o_ref[...] = _layer_norm(ALPHA * res_ref[...] + y, g_ref[...], b_ref[...])


def _mm_res_ln(a, w, res, g, b):
    m, k = a.shape
    d = w.shape[1]
    tm = _row_tile(m, 256)
    return pl.pallas_call(
        _mm_res_ln_kernel,
        grid=(m // tm,),
        in_specs=[pl.BlockSpec((tm, k), lambda i: (i, 0)),
                  pl.BlockSpec((k, d), lambda i: (0, 0)),
                  pl.BlockSpec((tm, d), lambda i: (i, 0)),
                  pl.BlockSpec((1, d), lambda i: (0, 0)),
                  pl.BlockSpec((1, d), lambda i: (0, 0))],
        out_specs=pl.BlockSpec((tm, d), lambda i: (i, 0)),
        out_shape=jax.ShapeDtypeStruct((m, d), F32),
        compiler_params=_cparams(("parallel",)),
        name="mm_res_ln",
    )(a, w, res, g, b)


def _route_kernel(x_ref, w_ref, b_ref, o_ref):
    logits = jnp.dot(x_ref[...], w_ref[...], preferred_element_type=F32,
                     precision=lax.Precision.HIGHEST) + b_ref[...]
    tm = logits.shape[0]
    lane = lax.broadcasted_iota(jnp.int32, (tm, LANES), 1)
    lanef = lane.astype(F32)
    big = float(LANES)
    ninf = -jnp.inf
    is_grp = (lane >= N_EXPERTS) & (lane < N_EXPERTS + N_GROUPS)
    lg = jnp.where(is_grp, logits, ninf)
    mg = jnp.max(lg, axis=-1, keepdims=True)
    p_top = 1.0 / jnp.sum(jnp.where(is_grp, jnp.exp(lg - mg), 0.0), axis=-1, keepdims=True)
    g_sel = jnp.min(jnp.where(lg == mg, lanef, big), axis=-1, keepdims=True) - float(N_EXPERTS)
    in_grp = (lane < N_EXPERTS) & ((lane >> EPG_SHIFT).astype(F32) == g_sel)
    le = jnp.where(in_grp, logits, ninf)
    v1 = jnp.max(le, axis=-1, keepdims=True)
    j1 = jnp.min(jnp.where(le == v1, lanef, big), axis=-1, keepdims=True)
    le2 = jnp.where(lanef == j1, ninf, le)
    v2 = jnp.max(le2, axis=-1, keepdims=True)
    j2 = jnp.min(jnp.where(le2 == v2, lanef, big), axis=-1, keepdims=True)
    e2 = jnp.exp(v2 - v1)
    den = 1.0 + e2
    w1 = (1.0 / den) * p_top
    w2 = (e2 / den) * p_top
    out = jnp.where(lane == 0, j1, jnp.where(lane == 1, j2, jnp.where(lane == 2, w1, jnp.where(lane == 3, w2, 0.0))))
    o_ref[...] = out


def _route(x, w_cat, b_cat):
    m, d = x.shape
    tm = _row_tile(m, 256)
    return pl.pallas_call(
        _route_kernel,
        grid=(m // tm,),
        in_specs=[pl.BlockSpec((tm, d), lambda i: (i, 0)),
                  pl.BlockSpec((d, LANES), lambda i: (0, 0)),
                  pl.BlockSpec((1, LANES), lambda i: (0, 0))],
        out_specs=pl.BlockSpec((tm, LANES), lambda i: (i, 0)),
        out_shape=jax.ShapeDtypeStruct((m, LANES), F32),
        compiler_params=_cparams(("parallel",)),
        name="moe_route",
    )(x, w_cat, b_cat)


def _moe_kernel(te_ref, ns_ref, st_ref, tok_ref, x_hbm, wt_ref, wg_ref, wu_ref, wd_ref, o_ref, xbuf, sem):
    i = pl.program_id(0)
    f = pl.program_id(1)
    nsub = ns_ref[i]

    def row_copy(tok, r):
        return pltpu.make_async_copy(x_hbm.at[pl.ds(tok, 1)], xbuf.at[pl.ds(r, 1)], sem)

    @pl.when(f == 0)
    def _():
        o_ref[...] = jnp.zeros_like(o_ref)
        base = st_ref[i] * MOE_SUB
        nrows = nsub * MOE_SUB

        def issue(r8, carry):
            for u in range(SUBLANES):
                r = r8 * SUBLANES + u
                row_copy(tok_ref[base + r], r).start()
            return carry

        def drain(r8, carry):
            for u in range(SUBLANES):
                row_copy(0, r8 * SUBLANES + u).wait()
            return carry

        lax.fori_loop(0, nrows // SUBLANES, issue, 0)
        lax.fori_loop(0, nrows // SUBLANES, drain, 0)

    @pl.when(nsub > 0)
    def _():
        xs0 = xbuf[0:MOE_SUB, :].astype(BF)
        wg = wg_ref[...].astype(BF)
        hg0 = jnp.dot(xs0, wg, preferred_element_type=F32)
        wu = wu_ref[...].astype(BF)
        hu0 = jnp.dot(xs0, wu, preferred_element_type=F32)
        wd = wd_ref[...].astype(BF)
        h0 = (jax.nn.silu(hg0) * hu0) * wt_ref[0:MOE_SUB, :]
        o_ref[0:MOE_SUB, :] += jnp.dot(h0.astype(BF), wd, preferred_element_type=F32)

        def body(s, carry):
            r0 = pl.multiple_of(s * MOE_SUB, MOE_SUB)
            xs = xbuf[pl.ds(r0, MOE_SUB), :].astype(BF)
            hg = jnp.dot(xs, wg, preferred_element_type=F32)
            hu = jnp.dot(xs, wu, preferred_element_type=F32)
            h = (jax.nn.silu(hg) * hu) * wt_ref[pl.ds(r0, MOE_SUB), :]
            o_ref[pl.ds(r0, MOE_SUB), :] += jnp.dot(h.astype(BF), wd, preferred_element_type=F32)
            return carry

        lax.fori_loop(1, nsub, body, 0)


def _moe_ffn(tile_expert, tile_nsub, tile_start, tok_pad, x, wt, w_gate, w_up, w_down, layer):
    _, d = x.shape
    nt = tile_expert.shape[0]
    ff = w_gate.shape[3]
    grid_spec = pltpu.PrefetchScalarGridSpec(
        num_scalar_prefetch=4,
        grid=(nt, ff // MOE_TF),
        in_specs=[pl.BlockSpec(memory_space=pl.ANY),
                  pl.BlockSpec((pl.Element(MOE_TM), pl.Element(1)), lambda i, f, te, ns, st, tk: (st[i] * MOE_SUB, 0)),
                  pl.BlockSpec((None, None, d, MOE_TF), lambda i, f, te, ns, st, tk: (layer, te[i], 0, f)),
                  pl.BlockSpec((None, None, d, MOE_TF), lambda i, f, te, ns, st, tk: (layer, te[i], 0, f)),
                  pl.BlockSpec((None, None, MOE_TF, d), lambda i, f, te, ns, st, tk: (layer, te[i], f, 0))],
        out_specs=pl.BlockSpec((MOE_TM, d), lambda i, f, te, ns, st, tk: (i, 0)),
        scratch_shapes=[pltpu.VMEM((MOE_TM, d), F32), pltpu.SemaphoreType.DMA(())],
    )
    return pl.pallas_call(
        _moe_kernel,
        grid_spec=grid_spec,
        out_shape=jax.ShapeDtypeStruct((nt * MOE_TM, d), F32),
        compiler_params=_cparams(("arbitrary", "arbitrary")),
        name="moe_ffn",
    )(tile_expert, tile_nsub, tile_start, tok_pad, x, wt, w_gate, w_up, w_down)


def _ln_ple_kernel(x1_ref, y0_ref, y1_ref, g_ref, b_ref, wg_ref, p_ref, wp_ref, o_ref):
    z = ALPHA * x1_ref[...] + (y0_ref[...] + y1_ref[...])
    x2 = _layer_norm(z, g_ref[...], b_ref[...])
    gate = jax.nn.sigmoid(jnp.dot(x2.astype(BF), wg_ref[...], preferred_element_type=F32))
    proj = jnp.dot(p_ref[...].astype(BF), wp_ref[...], preferred_element_type=F32)
    o_ref[...] = x2 + gate * proj


def _ln_ple(x1, y0, y1, g, b, w_gate, p, w_proj):
    m, d = x1.shape
    pd = p.shape[1]
    tm = _row_tile(m, 256)
    row = lambda w: pl.BlockSpec((tm, w), lambda i: (i, 0))
    return pl.pallas_call(
        _ln_ple_kernel,
        grid=(m // tm,),
        in_specs=[row(d), row(d), row(d),
                  pl.BlockSpec((1, d), lambda i: (0, 0)),
                  pl.BlockSpec((1, d), lambda i: (0, 0)),
                  pl.BlockSpec((d, d), lambda i: (0, 0)),
                  row(pd),
                  pl.BlockSpec((pd, d), lambda i: (0, 0))],
        out_specs=row(d),
        out_shape=jax.ShapeDtypeStruct((m, d), F32),
        compiler_params=_cparams(("parallel",)),
        name="ln_ple",
    )(x1, y0, y1, g, b, w_gate, p, w_proj)


def _moe_layer(x1, w_route, b_route, w_gate, w_up, w_down, layer):
    n, d = x1.shape
    routed = _route(x1, w_route, b_route)
    e_tok = routed[:, 0:2].astype(jnp.int32)
    w_tok = routed[:, 2:4]
    a = 2 * n
    sub_per_tile = MOE_TM // MOE_SUB
    nt = a // MOE_TM + N_EXPERTS
    n_sub_max = a // MOE_SUB + N_EXPERTS
    e_flat = e_tok.reshape(a)
    onehot = (e_flat[:, None] == jnp.arange(N_EXPERTS, dtype=jnp.int32)[None, :]).astype(jnp.int32)
    csum = jnp.cumsum(onehot, axis=0)
    counts = csum[a - 1]
    rank = jnp.sum(csum * onehot, axis=1) - 1
    sub_per = (counts + MOE_SUB - 1) // MOE_SUB
    sub_off = jnp.cumsum(sub_per) - sub_per
    tiles_per = (counts + MOE_TM - 1) // MOE_TM
    tile_end = jnp.cumsum(tiles_per)
    tile_off = tile_end - tiles_per
    src_row = sub_off[e_flat] * MOE_SUB + rank
    dst_row = (tile_off[e_flat] + rank // MOE_TM) * MOE_TM + rank % MOE_TM
    rows_in = (n_sub_max + sub_per_tile) * MOE_SUB
    asg_pad = jnp.full((rows_in,), -1, jnp.int32).at[src_row].set(jnp.arange(a, dtype=jnp.int32))
    asg = jnp.maximum(asg_pad, 0)
    tok_pad = jnp.where(asg_pad >= 0, asg // 2, 0)
    w_pad = jnp.where(asg_pad >= 0, w_tok.reshape(a)[asg], 0.0)
    tile_id = jnp.arange(nt, dtype=jnp.int32)
    n_used = tile_end[N_EXPERTS - 1]
    tile_expert = jnp.minimum(jnp.sum((tile_end[None, :] <= tile_id[:, None]).astype(jnp.int32), axis=1),
                              N_EXPERTS - 1)
    k_in_expert = tile_id - tile_off[tile_expert]
    used = tile_id < n_used
    tile_nsub = jnp.where(used, jnp.clip(sub_per[tile_expert] - k_in_expert * sub_per_tile, 0, sub_per_tile), 0)
    tile_start = jnp.where(used, sub_off[tile_expert] + k_in_expert * sub_per_tile, 0)
    last_used = jnp.sum(jnp.where(tile_id == n_used - 1, tile_expert, 0))
    tile_expert = jnp.where(used, tile_expert, last_used)
    yg = _moe_ffn(tile_expert.astype(jnp.int32), tile_nsub.astype(jnp.int32), tile_start.astype(jnp.int32),
                  tok_pad, x1, w_pad.reshape(-1, 1), w_gate, w_up, w_down, layer)
    dst2 = dst_row.reshape(n, 2)
    return (yg.at[dst2[:, 0]].get(mode="promise_in_bounds"), yg.at[dst2[:, 1]].get(mode="promise_in_bounds"))


def _nsa_in_kernel(x_ref, w_ref, cos_ref, sin_ref, o_ref, ob_ref, *, n_q_blocks):
    j = pl.program_id(1)
    u = jnp.dot(x_ref[...].astype(BF), w_ref[...], preferred_element_type=F32)
    cos = cos_ref[...]
    sin = sin_ref[...]
    heads = u.shape[1] // HEAD_DIM
    for h in range(heads):
        seg = u[:, h * HEAD_DIM:(h + 1) * HEAD_DIM]
        rot = seg * cos + pltpu.roll(seg, HEAD_DIM // 2, 1) * sin
        if h < heads // 2:
            val = rot
        else:
            val = jnp.where(j < n_q_blocks, rot, seg)
        o_ref[:, h * HEAD_DIM:(h + 1) * HEAD_DIM] = val

    @pl.when(j > n_q_blocks)
    def _():
        ob_ref[...] = o_ref[...].astype(BF)


def _nsa_in(xb, w, cos, sin_signed):
    m, k = xb.shape
    n = w.shape[1]
    tn = 2 * N_KV * HEAD_DIM
    tm = _row_tile(m, 768)
    nqb = (N_HEADS * HEAD_DIM) // tn
    return pl.pallas_call(
        functools.partial(_nsa_in_kernel, n_q_blocks=nqb),
        grid=(m // tm, n // tn),
        in_specs=[pl.BlockSpec((tm, k), lambda i, j: (i, 0)),
                  pl.BlockSpec((k, tn), lambda i, j: (0, j)),
                  pl.BlockSpec((tm, HEAD_DIM), lambda i, j: (i, 0)),
                  pl.BlockSpec((tm, HEAD_DIM), lambda i, j: (i, 0))],
        out_specs=[pl.BlockSpec((tm, tn), lambda i, j: (i, j)),
                   pl.BlockSpec((tm, tn), lambda i, j: (i, jnp.maximum(j - (nqb + 1), 0)))],
        out_shape=[jax.ShapeDtypeStruct((m, n), F32), jax.ShapeDtypeStruct((m, n - (nqb + 1) * tn), BF)],
        compiler_params=_cparams(("parallel", "arbitrary")),
        name="nsa_in",
    )(xb, w, cos, sin_signed)


def _compress_kernel(rows_ref, w1_ref, pe_ref, w1f_ref, w2_ref, o_ref, a_ref):
    t = rows_ref.shape[0]
    nch = t // CMP_STRIDE
    for s in range(CMP_STRIDE):
        a_ref[:, s * HEAD_DIM:(s + 1) * HEAD_DIM] = rows_ref[pl.ds(s, nch, stride=CMP_STRIDE), :].astype(BF)
    p = jnp.dot(a_ref[...], w1_ref[...], preferred_element_type=F32)
    pe_term = jnp.dot(pe_ref[...].astype(BF), w1f_ref[...], preferred_element_type=F32)[0:1, :]
    pre = p[:, :HEAD_DIM] + pltpu.roll(p[:, HEAD_DIM:], nch - 1, 0) + pe_term
    o_ref[...] = jnp.dot(jax.nn.gelu(pre).astype(BF), w2_ref[...], preferred_element_type=F32).astype(o_ref.dtype)


def _compress(rows, w1cat, pe8, w1flat, w2):
    nseq, t, _ = rows.shape
    nch = t // CMP_STRIDE
    kd = CMP_STRIDE * HEAD_DIM
    return pl.pallas_call(
        _compress_kernel,
        grid=(nseq, 2, N_KV),
        in_specs=[pl.BlockSpec((None, t, HEAD_DIM), lambda b, w, g: (b, 0, w * N_KV + g)),
                  pl.BlockSpec((None, kd, 2 * HEAD_DIM), lambda b, w, g: (w, 0, 0)),
                  pl.BlockSpec((None, SUBLANES, 2 * kd), lambda b, w, g: (w, 0, 0)),
                  pl.BlockSpec((None, 2 * kd, HEAD_DIM), lambda b, w, g: (w, 0, 0)),
                  pl.BlockSpec((None, HEAD_DIM, HEAD_DIM), lambda b, w, g: (w, 0, 0))],
        out_specs=pl.BlockSpec((None, None, None, nch, HEAD_DIM), lambda b, w, g: (b, w, g, 0, 0)),
        out_shape=jax.ShapeDtypeStruct((nseq, 2, N_KV, nch, HEAD_DIM), BF),
        scratch_shapes=[pltpu.VMEM((nch, kd), BF)],
        compiler_params=_cparams(("parallel", "arbitrary", "arbitrary")),
        name="compress",
    )(rows, w1cat, pe8, w1flat, w2)


def _stack_heads(qblk, tq):
    return jnp.concatenate([qblk[:, h * HEAD_DIM:(h + 1) * HEAD_DIM] for h in range(HPG)], axis=0)


def _cmp_attn_kernel(q_ref, kc_ref, vc_ref, oc_ref, sb_ref, *, tq, nsp, ns, nc, pos0):
    q0 = pos0 + pl.program_id(1) * tq
    _cmp_attn_body(q_ref, [kc_ref[g] for g in range(N_KV)], [vc_ref[g] for g in range(N_KV)], oc_ref, sb_ref, q0,
                   tq=tq, nsp=nsp, ns=ns, nc=nc)


def _cmp_attn_body(q_ref, kcs, vcs, oc_ref, sb_ref, q0, *, tq, nsp, ns, nc):
    ncp = kcs[0].shape[0]
    rows = HPG * tq
    gw = HPG * HEAD_DIM
    nt = (((1,), (1,)), ((), ()))
    qs = [_stack_heads(q_ref[:, g * gw:(g + 1) * gw], tq).astype(BF) for g in range(N_KV)]
    ss = [lax.dot_general(qs[g], kcs[g], nt, preferred_element_type=F32) * SCALE for g in range(N_KV)]
    t = q0 + (lax.broadcasted_iota(jnp.int32, (rows, ncp), 0) & (tq - 1))
    n = lax.broadcasted_iota(jnp.int32, (rows, ncp), 1)
    mask = (n * CMP_STRIDE + (CMP_BLOCK - 1) <= t) & (n < nc)
    ps = []
    for g in range(N_KV):
        sm = jnp.where(mask, ss[g], NEG)
        m = jnp.max(sm, axis=-1, keepdims=True)
        e = jnp.where(mask, jnp.exp(sm - m), 0.0)
        l = jnp.sum(e, axis=-1, keepdims=True)
        ps.append(e / jnp.where(l > 0.0, l, 1.0))
    ocs = [jnp.dot(ps[g].astype(BF), vcs[g], preferred_element_type=F32) for g in range(N_KV)]
    for g in range(N_KV):
        for h in range(HPG):
            c0 = (g * HPG + h) * HEAD_DIM
            oc_ref[:, c0:c0 + HEAD_DIM] = ocs[g][h * tq:(h + 1) * tq]
    nn = lax.broadcasted_iota(jnp.int32, (ncp, nsp), 0)
    bb = lax.broadcasted_iota(jnp.int32, (ncp, nsp), 1)
    dd = nn - bb * (SEL_BLOCK // CMP_STRIDE)
    ov = ((dd >= 1 - CMP_BLOCK // CMP_STRIDE) & (dd < SEL_BLOCK // CMP_STRIDE) & (nn < nc)).astype(BF)
    bcol = lax.broadcasted_iota(jnp.int32, (tq, nsp), 1)
    bcolf = bcol.astype(F32)
    tcol = q0 + lax.broadcasted_iota(jnp.int32, (tq, nsp), 0)
    causal = bcol * SEL_BLOCK <= tcol
    forced = (bcol == (tcol >> SEL_SHIFT)) | (bcol == 0)
    scores = []
    for g in range(N_KV):
        p = ps[g]
        imp = p[0:tq] + p[tq:2 * tq] + p[2 * tq:3 * tq] + p[3 * tq:4 * tq]
        hi = imp.astype(BF)
        r1 = imp - hi.astype(F32)
        mid = r1.astype(BF)
        lo = (r1 - mid.astype(F32)).astype(BF)
        imp_sel = (jnp.dot(hi, ov, preferred_element_type=F32) + jnp.dot(mid, ov, preferred_element_type=F32)
                   + jnp.dot(lo, ov, preferred_element_type=F32))
        score = jnp.where(causal, imp_sel + jnp.where(forced, FORCE_BONUS, 0.0), NEG)
        scores.append(jnp.where(bcol < ns, score, -3e38))
    sels = [jnp.zeros((tq, nsp), jnp.bool_) for _ in range(N_KV)]
    for _ in range(min(N_SEL, ns)):
        mxs = [jnp.max(scores[g], axis=-1, keepdims=True) for g in range(N_KV)]
        idxs = [jnp.min(jnp.where(scores[g] == mxs[g], bcolf, float(nsp)), axis=-1, keepdims=True)
                for g in range(N_KV)]
        for g in range(N_KV):
            pick = bcolf == idxs[g]
            sels[g] = sels[g] | pick
            scores[g] = jnp.where(pick, -jnp.inf, scores[g])
    for g in range(N_KV):
        sb_ref[:, g * nsp:(g + 1) * nsp] = jnp.where(sels[g], 0.0, NEG)


def _cmp_attn(u5, kcvc, *, nseq, tq_total, tq, row0, t_kv, pos0):
    ncp = kcvc.shape[3]
    nc = (t_kv - CMP_BLOCK) // CMP_STRIDE + 1
    ns = -(-t_kv // SEL_BLOCK)
    nsp = -(-ns // LANES) * LANES
    nqt = tq_total // tq
    rb0 = row0 // tq
    hd = N_HEADS * HEAD_DIM
    kern = functools.partial(_cmp_attn_kernel, tq=tq, nsp=nsp, ns=ns, nc=nc, pos0=pos0)
    return pl.pallas_call(
        kern,
        grid=(nseq, nqt),
        in_specs=[pl.BlockSpec((tq, hd), lambda b, i: (rb0 + b * nqt + i, 0)),
                  pl.BlockSpec((None, None, N_KV, ncp, HEAD_DIM), lambda b, i: (b, 0, 0, 0, 0)),
                  pl.BlockSpec((None, None, N_KV, ncp, HEAD_DIM), lambda b, i: (b, 1, 0, 0, 0))],
        out_specs=[pl.BlockSpec((tq, hd), lambda b, i: (b * nqt + i, 0)),
                   pl.BlockSpec((tq, N_KV * nsp), lambda b, i: (b * nqt + i, 0))],
        out_shape=[jax.ShapeDtypeStruct((nseq * tq_total, hd), F32),
                   jax.ShapeDtypeStruct((nseq * tq_total, N_KV * nsp), F32)],
        compiler_params=_cparams(("parallel", "arbitrary")),
        name="cmp_attn",
    )(u5, kcvc, kcvc)


EXP2_SCALE = SCALE * 1.4426950408889634


def _sel_win_kernel(q_ref, sb_ref, gt_ref, oc_ref, ks_ref, vs_ref, kw_ref, vw_ref, o_ref,
                    m_ref, acc_ref, s_ref, *, tq, nsp, tk):
    g = pl.program_id(1)
    q0 = pl.program_id(2) * tq
    rows = HPG * tq
    nt = (((1,), (1,)), ((), ()))
    q = _stack_heads(q_ref[...], tq).astype(BF)
    bias = jnp.concatenate([sb_ref[...]] * HPG, axis=0).astype(BF)
    qa = jnp.concatenate([q, bias], axis=1)
    ones = jnp.ones((tk, HEAD_DIM), BF)

    m_ref[...] = jnp.full(m_ref.shape, -jnp.inf, F32)
    acc_ref[...] = jnp.zeros(acc_ref.shape, F32)

    def scores(j):
        k0 = pl.multiple_of(j * tk, tk)
        blk = (k0 + lax.broadcasted_iota(jnp.int32, (tk, nsp), 0)) >> SEL_SHIFT
        onehot = (blk == lax.broadcasted_iota(jnp.int32, (tk, nsp), 1)).astype(BF)
        ka = jnp.concatenate([ks_ref[pl.ds(k0, tk), :], onehot], axis=1)
        return [lax.dot_general(qa[h * tq:(h + 1) * tq], ka, nt, preferred_element_type=F32) for h in range(HPG)]

    def consume(ss, j, masked):
        k0 = pl.multiple_of(j * tk, tk)
        va = jnp.concatenate([vs_ref[pl.ds(k0, tk), :], ones], axis=1)
        if masked:
            ok = (k0 + lax.broadcasted_iota(jnp.int32, (tq, tk), 1)
                  <= q0 + lax.broadcasted_iota(jnp.int32, (tq, tk), 0))
            ss = [jnp.where(ok, sh, NEG) for sh in ss]
        ps, alphas = [], []
        for h in range(HPG):
            rs = slice(h * tq, (h + 1) * tq)
            m_old = m_ref[rs]
            m_new = jnp.maximum(m_old, jnp.max(ss[h], axis=-1, keepdims=True))
            alphas.append(jnp.exp2((m_old - m_new) * EXP2_SCALE))
            ps.append(jnp.exp2((ss[h] - m_new) * EXP2_SCALE).astype(BF))
            m_ref[rs] = m_new
        for h in range(HPG):
            rs = slice(h * tq, (h + 1) * tq)
            acc_ref[rs] = alphas[h] * acc_ref[rs] + jnp.dot(ps[h], va, preferred_element_type=F32)

    j_last = (q0 + tq - 1) // tk
    for h, sh in enumerate(scores(0)):
        s_ref[h] = sh

    def sel_body(j, carry):
        cur = [s_ref[h] for h in range(HPG)]
        nxt = scores(j + 1)
        consume(cur, j, False)
        for h in range(HPG):
            s_ref[h] = nxt[h]
        return carry

    lax.fori_loop(0, j_last, sel_body, 0)
    nkw = WINDOW + tq
    w0 = pl.multiple_of(jnp.maximum(q0 - WINDOW, 0), tq)
    sw = lax.dot_general(q, kw_ref[pl.ds(w0, nkw), :], nt, preferred_element_type=F32)
    consume([s_ref[h] for h in range(HPG)], j_last, True)
    acc = acc_ref[...]
    o_s = acc[:, :HEAD_DIM] / acc[:, HEAD_DIM:HEAD_DIM + 1]
    vwa = jnp.concatenate([vw_ref[pl.ds(w0, nkw), :], jnp.ones((nkw, HEAD_DIM), BF)], axis=1)
    t = q0 + (lax.broadcasted_iota(jnp.int32, (rows, nkw), 0) & (tq - 1))
    kpos = w0 + lax.broadcasted_iota(jnp.int32, (rows, nkw), 1)
    sw = jnp.where((kpos <= t) & (kpos > t - WINDOW), sw, NEG)
    pw = jnp.exp2((sw - jnp.max(sw, axis=-1, keepdims=True)) * EXP2_SCALE)
    aw = jnp.dot(pw.astype(BF), vwa, preferred_element_type=F32)
    o_w = aw[:, :HEAD_DIM] / aw[:, HEAD_DIM:HEAD_DIM + 1]

    gts = gt_ref[...]
    oc = oc_ref[...]
    lane = lax.broadcasted_iota(jnp.int32, gts.shape, 1)
    for h in range(HPG):
        base = (g * HPG + h) * 3
        gc = jnp.sum(jnp.where(lane == base, gts, 0.0), axis=-1, keepdims=True)
        gs = jnp.sum(jnp.where(lane == base + 1, gts, 0.0), axis=-1, keepdims=True)
        gw = jnp.sum(jnp.where(lane == base + 2, gts, 0.0), axis=-1, keepdims=True)
        sl = slice(h * HEAD_DIM, (h + 1) * HEAD_DIM)
        rs = slice(h * tq, (h + 1) * tq)
        o_ref[:, sl] = gc * oc[:, sl] + gs * o_s[rs] + gw * o_w[rs]


def _sel_win_attn(u5, sbias, gates, oc, ksvs, kwvw, *, tq, tk):
    t = ksvs.shape[0]
    nsp = sbias.shape[1] // N_KV
    nqt = t // tq
    gw = HPG * HEAD_DIM
    assert t % tk == 0 and t % tq == 0 and t >= WINDOW + tq and WINDOW % tq == 0
    kern = functools.partial(_sel_win_kernel, tq=tq, nsp=nsp, tk=tk)
    return pl.pallas_call(
        kern,
        grid=(1, N_KV, nqt),
        in_specs=[pl.BlockSpec((tq, gw), lambda b, g, i: (i, g)),
                  pl.BlockSpec((tq, nsp), lambda b, g, i: (i, g)),
                  pl.BlockSpec((tq, LANES), lambda b, g, i: (i, 0)),
                  pl.BlockSpec((tq, gw), lambda b, g, i: (i, g)),
                  pl.BlockSpec((t, HEAD_DIM), lambda b, g, i: (0, g)),
                  pl.BlockSpec((t, HEAD_DIM), lambda b, g, i: (0, N_KV + g)),
                  pl.BlockSpec((t, HEAD_DIM), lambda b, g, i: (0, g)),
                  pl.BlockSpec((t, HEAD_DIM), lambda b, g, i: (0, N_KV + g))],
        out_specs=pl.BlockSpec((tq, gw), lambda b, g, i: (i, g)),
        out_shape=jax.ShapeDtypeStruct((t, N_KV * gw), F32),
        scratch_shapes=[pltpu.VMEM((HPG * tq, 1), F32), pltpu.VMEM((HPG * tq, 2 * HEAD_DIM), F32),
                        pltpu.VMEM((HPG, tq, tk), F32)],
        compiler_params=_cparams(("parallel", "arbitrary", "arbitrary")),
        name="sel_win_prompt",
    )(u5, sbias, gates, oc, ksvs, ksvs, kwvw, kwvw)


ROWS_PER_TOKEN = 4 * N_KV
CMP_PAGES_PER_STEP = 8


def _compress_paged_kernel(pt_ref, *refs, pg, page_size):
    page_refs = refs[:pg]
    w1_ref, o_ref, a_ref = refs[pg:]
    cpp = page_size // CMP_STRIDE
    for k in range(pg):
        xc = jnp.swapaxes(page_refs[k][...], 0, 1)
        for wg in range(2 * N_KV):
            w, g = divmod(wg, N_KV)
            r0 = (g * pg + k) * cpp
            ys = jnp.swapaxes(xc[wg].reshape(cpp, CMP_STRIDE, HEAD_DIM), 0, 1)
            for s in range(CMP_STRIDE):
                a_ref[w, r0:r0 + cpp, s * HEAD_DIM:(s + 1) * HEAD_DIM] = ys[s]
    for w in range(2):
        p = jnp.dot(a_ref[w].astype(BF), w1_ref[w], preferred_element_type=F32)
        for g in range(N_KV):
            o_ref[w, g] = p[g * pg * cpp:(g + 1) * pg * cpp]


def _compress_paged(page_table, cache_tok, w1cat, *, page_size):
    bs, n_pages = page_table.shape
    pg = CMP_PAGES_PER_STEP
    assert n_pages % pg == 0
    cpp = page_size // CMP_STRIDE
    kd = CMP_STRIDE * HEAD_DIM
    page_spec = lambda k: pl.BlockSpec((page_size, 2 * N_KV, HEAD_DIM), lambda b, j, pt: (pt[b, j * pg + k], 0, 0))
    grid_spec = pltpu.PrefetchScalarGridSpec(
        num_scalar_prefetch=1,
        grid=(bs, n_pages // pg),
        in_specs=[page_spec(k) for k in range(pg)] + [pl.BlockSpec((2, kd, 2 * HEAD_DIM), lambda b, j, pt: (0, 0, 0))],
        out_specs=pl.BlockSpec((None, 2, N_KV, pg * cpp, 2 * HEAD_DIM), lambda b, j, pt: (b, 0, 0, j, 0)),
        scratch_shapes=[pltpu.VMEM((2, N_KV * pg * cpp, kd), F32)],
    )
    return pl.pallas_call(
        functools.partial(_compress_paged_kernel, pg=pg, page_size=page_size),
        grid_spec=grid_spec,
        out_shape=jax.ShapeDtypeStruct((bs, 2, N_KV, n_pages * cpp, 2 * HEAD_DIM), F32),
        compiler_params=_cparams(("parallel", "arbitrary")),
        name="compress_paged",
    )(page_table, *([cache_tok] * pg), w1cat)


def _compress_finish(p, pe_term, w2):
    nch = p.shape[0]
    pre = p[:, :HEAD_DIM] + pltpu.roll(p[:, HEAD_DIM:], nch - 1, 0) + pe_term
    return jnp.dot(jax.nn.gelu(pre).astype(BF), w2, preferred_element_type=F32).astype(BF)


def _cmp_attn_paged_kernel(q_ref, pk_ref, pv_ref, pe_ref, w1f_ref, w2_ref, oc_ref, sb_ref, *, tq, nsp, ns, nc, pos0):
    pe_k = jnp.dot(pe_ref[0].astype(BF), w1f_ref[0], preferred_element_type=F32)[0:1, :]
    pe_v = jnp.dot(pe_ref[1].astype(BF), w1f_ref[1], preferred_element_type=F32)[0:1, :]
    kcs = [_compress_finish(pk_ref[g], pe_k, w2_ref[0]) for g in range(N_KV)]
    vcs = [_compress_finish(pv_ref[g], pe_v, w2_ref[1]) for g in range(N_KV)]
    _cmp_attn_body(q_ref, kcs, vcs, oc_ref, sb_ref, pos0, tq=tq, nsp=nsp, ns=ns, nc=nc)


def _cmp_attn_paged(u5, pcmp, pe8, w1flat, w2, *, tq, row0, t_kv, pos0):
    bs, _, _, nch, _ = pcmp.shape
    nc = (t_kv - CMP_BLOCK) // CMP_STRIDE + 1
    ns = -(-t_kv // SEL_BLOCK)
    nsp = -(-ns // LANES) * LANES
    rb0 = row0 // tq
    hd = N_HEADS * HEAD_DIM
    kd = CMP_STRIDE * HEAD_DIM
    kern = functools.partial(_cmp_attn_paged_kernel, tq=tq, nsp=nsp, ns=ns, nc=nc, pos0=pos0)
    return pl.pallas_call(
        kern,
        grid=(bs,),
        in_specs=[pl.BlockSpec((tq, hd), lambda b: (rb0 + b, 0)),
                  pl.BlockSpec((None, None, N_KV, nch, 2 * HEAD_DIM), lambda b: (b, 0, 0, 0, 0)),
                  pl.BlockSpec((None, None, N_KV, nch, 2 * HEAD_DIM), lambda b: (b, 1, 0, 0, 0)),
                  pl.BlockSpec((2, SUBLANES, 2 * kd), lambda b: (0, 0, 0)),
                  pl.BlockSpec((2, 2 * kd, HEAD_DIM), lambda b: (0, 0, 0)),
                  pl.BlockSpec((2, HEAD_DIM, HEAD_DIM), lambda b: (0, 0, 0))],
        out_specs=[pl.BlockSpec((tq, hd), lambda b: (b, 0)),
                   pl.BlockSpec((tq, N_KV * nsp), lambda b: (b, 0))],
        out_shape=[jax.ShapeDtypeStruct((bs * tq, hd), F32),
                   jax.ShapeDtypeStruct((bs * tq, N_KV * nsp), F32)],
        compiler_params=_cparams(("parallel",)),
        name="cmp_attn_paged",
    )(u5, pcmp, pcmp, pe8, w1flat, w2)


SEL_PAGES_PER_STEP = 8


def _sel_win_paged_kernel(pt_ref, *refs, pps, n_pages, page_size, nsp, tq, win_keep, past_len):
    page_refs = refs[:pps]
    wq_ref, newkv_ref, wcache_ref, newwin_ref, gt_ref, oc_ref, o_ref, m_ref, l_ref, acc_ref = refs[pps:]
    j = pl.program_id(1)
    n_steps = n_pages // pps
    rows = N_HEADS * tq
    kvd = N_KV * HEAD_DIM
    wq = wq_ref[...]
    nt = (((1,), (1,)), ((), ()))

    @pl.when(j == 0)
    def _():
        m_ref[...] = jnp.full(m_ref.shape, -jnp.inf, F32)
        l_ref[...] = jnp.zeros(l_ref.shape, F32)
        acc_ref[...] = jnp.zeros(acc_ref.shape, F32)

    def tile(k_all, v_all, first_block, mask):
        nk = k_all.shape[0]
        blk = first_block + (lax.broadcasted_iota(jnp.int32, (nk, nsp), 0) >> SEL_SHIFT)
        onehot = (blk == lax.broadcasted_iota(jnp.int32, (nk, nsp), 1)).astype(BF)
        s = lax.dot_general(wq, jnp.concatenate([k_all, onehot], axis=1), nt, preferred_element_type=F32)
        if mask is not None:
            s = jnp.where(mask, s, NEG)
        m_old = m_ref[...]
        m_new = jnp.maximum(m_old, jnp.max(s, axis=-1, keepdims=True))
        alpha = jnp.exp2((m_old - m_new) * EXP2_SCALE)
        p = jnp.exp2((s - m_new) * EXP2_SCALE)
        l_ref[...] = alpha * l_ref[...] + jnp.sum(p, axis=-1, keepdims=True)
        acc_ref[...] = alpha * acc_ref[...] + jnp.dot(p.astype(BF), v_all, preferred_element_type=F32)
        m_ref[...] = m_new

    def page_rows(which):
        tr = [jnp.swapaxes(ref[...], 0, 1) for ref in page_refs]
        return jnp.concatenate(
            [jnp.concatenate([t[which * N_KV + g] for g in range(N_KV)], axis=1) for t in tr],
            axis=0).astype(BF)

    @pl.when(j < n_steps)
    def _():
        tile(page_rows(0), page_rows(1), j * (pps * page_size // SEL_BLOCK), None)

    @pl.when(j == n_steps)
    def _():
        qi = lax.broadcasted_iota(jnp.int32, (rows, page_size), 0) & (tq - 1)
        key = lax.broadcasted_iota(jnp.int32, (rows, page_size), 1)
        nk = newkv_ref[...]
        zpad = jnp.zeros((page_size - tq, kvd), F32)
        k_all = jnp.concatenate([nk[:, :kvd], zpad], axis=0).astype(BF)
        v_all = jnp.concatenate([nk[:, kvd:], zpad], axis=0).astype(BF)
        tile(k_all, v_all, n_pages * (page_size // SEL_BLOCK), key <= qi)
        o_s = acc_ref[...] / l_ref[...]

        nw = newwin_ref[...]
        kc = jnp.concatenate([wcache_ref[pl.ds(g, win_keep, stride=2 * N_KV), :] for g in range(N_KV)], axis=1)
        vc = jnp.concatenate([wcache_ref[pl.ds(N_KV + g, win_keep, stride=2 * N_KV), :] for g in range(N_KV)], axis=1)
        kw = jnp.concatenate([kc, nw[:, :kvd], zpad], axis=0).astype(BF)
        vw = jnp.concatenate([vc, nw[:, kvd:], zpad], axis=0).astype(BF)
        nkw = win_keep + page_size
        sw = lax.dot_general(wq[:, :kvd], kw, nt, preferred_element_type=F32)
        t = past_len + (lax.broadcasted_iota(jnp.int32, (rows, nkw), 0) & (tq - 1))
        kpos = (past_len - win_keep) + lax.broadcasted_iota(jnp.int32, (rows, nkw), 1)
        sw = jnp.where((kpos <= t) & (kpos > t - WINDOW), sw, NEG)
        pw = jnp.exp2((sw - jnp.max(sw, axis=-1, keepdims=True)) * EXP2_SCALE)
        o_w = jnp.dot(pw.astype(BF), vw, preferred_element_type=F32) / jnp.sum(pw, axis=-1, keepdims=True)

        gts = gt_ref[...]
        oc = oc_ref[...]
        lane = lax.broadcasted_iota(jnp.int32, gts.shape, 1)
        for hh in range(N_HEADS):
            g = hh // HPG
            gc = jnp.sum(jnp.where(lane == hh * 3, gts, 0.0), axis=-1, keepdims=True)
            gs = jnp.sum(jnp.where(lane == hh * 3 + 1, gts, 0.0), axis=-1, keepdims=True)
            gw = jnp.sum(jnp.where(lane == hh * 3 + 2, gts, 0.0), axis=-1, keepdims=True)
            rs = slice(hh * tq, (hh + 1) * tq)
            cs = slice(g * HEAD_DIM, (g + 1) * HEAD_DIM)
            os_ = slice(hh * HEAD_DIM, (hh + 1) * HEAD_DIM)
            o_ref[:, os_] = gc * oc[:, os_] + gs * o_s[rs, cs] + gw * o_w[rs, cs]


def _sel_win_paged(page_table, cache_tok, wq, u5, wcache_flat, gates, oc, *, tq, row0, page_size, win_keep, past_len):
    bs, n_pages = page_table.shape
    pps = SEL_PAGES_PER_STEP
    assert n_pages % pps == 0
    n_steps = n_pages // pps
    rows = N_HEADS * tq
    kvd = N_KV * HEAD_DIM
    nsp = wq.shape[2] - kvd
    rb0 = row0 // tq
    sel_col = (N_HEADS * HEAD_DIM + 2 * kvd) // (2 * kvd)
    kern = functools.partial(_sel_win_paged_kernel, pps=pps, n_pages=n_pages, page_size=page_size, nsp=nsp, tq=tq,
                             win_keep=win_keep, past_len=past_len)
    page_spec = lambda k: pl.BlockSpec(
        (page_size, 2 * N_KV, HEAD_DIM), lambda b, j, pt: (pt[b, jnp.minimum(j, n_steps - 1) * pps + k], 1, 0))
    grid_spec = pltpu.PrefetchScalarGridSpec(
        num_scalar_prefetch=1,
        grid=(bs, n_steps + 1),
        in_specs=[page_spec(k) for k in range(pps)] + [
            pl.BlockSpec((None, rows, kvd + nsp), lambda b, j, pt: (b, 0, 0)),
            pl.BlockSpec((tq, 2 * kvd), lambda b, j, pt: (rb0 + b, sel_col)),
            pl.BlockSpec((win_keep * 2 * N_KV, HEAD_DIM), lambda b, j, pt: (b, 0)),
            pl.BlockSpec((tq, 2 * kvd), lambda b, j, pt: (rb0 + b, sel_col + 1)),
            pl.BlockSpec((tq, LANES), lambda b, j, pt: (rb0 + b, 0)),
            pl.BlockSpec((tq, N_HEADS * HEAD_DIM), lambda b, j, pt: (b, 0))],
        out_specs=pl.BlockSpec((tq, N_HEADS * HEAD_DIM), lambda b, j, pt: (b, 0)),
        scratch_shapes=[pltpu.VMEM((rows, 1), F32), pltpu.VMEM((rows, 1), F32), pltpu.VMEM((rows, kvd), F32)],
    )
    return pl.pallas_call(
        kern,
        grid_spec=grid_spec,
        out_shape=jax.ShapeDtypeStruct((bs * tq, N_HEADS * HEAD_DIM), F32),
        compiler_params=_cparams(("parallel", "arbitrary")),
        name="sel_win_paged",
    )(page_table, *([cache_tok] * pps), wq, u5, wcache_flat, u5, gates, oc)


def _gelu(u):
    return jax.nn.gelu(u)


def _identity(u):
    return u


def kernel(x_prompt, x_sample, state_conv, state_h, cache_kv, cache_win, page_table, p_prompt, p_sample,
           lru_w_in, lru_conv_w, lru_conv_b, lru_w_a, lru_b_a, lru_w_x, lru_b_x, lru_lambda, lru_w_out,
           nsa_w_in, nsa_cmp_w1, nsa_cmp_pe, nsa_cmp_w2, nsa_w_out,
           ln1_g, ln1_b, ln2_g, ln2_b,
           moe_w_group, moe_b_group, moe_w_expert, moe_b_expert, moe_w_gate, moe_w_up, moe_w_down,
           ple_w_gate, ple_w_proj):
    bp, tp, d = x_prompt.shape
    bs, ts, _ = x_sample.shape
    assert bp == 1 and ts == SUBLANES
    n_p = bp * tp
    n_s = bs * ts
    n = n_p + n_s
    past_len = page_table.shape[1] * cache_kv.shape[2]
    d_rnn = lru_w_out.shape[1]
    hd = N_HEADS * HEAD_DIM
    kvw = 2 * N_KV * HEAD_DIM

    x0 = jnp.concatenate([x_prompt.reshape(n_p, d), x_sample.reshape(n_s, d)], axis=0)

    def moe_and_ple(x1, layer):
        w_route = jnp.zeros((d, LANES), F32).at[:, :N_EXPERTS].set(moe_w_expert[layer])
        w_route = w_route.at[:, N_EXPERTS:N_EXPERTS + N_GROUPS].set(moe_w_group[layer])
        b_route = jnp.zeros((1, LANES), F32).at[0, :N_EXPERTS].set(moe_b_expert[layer])
        b_route = b_route.at[0, N_EXPERTS:N_EXPERTS + N_GROUPS].set(moe_b_group[layer])
        y0, y1 = _moe_layer(x1, w_route, b_route, moe_w_gate, moe_w_up, moe_w_down, layer)
        p_cat = jnp.concatenate([p_prompt[layer].reshape(n_p, -1), p_sample[layer].reshape(n_s, -1)], axis=0)
        return _ln_ple(x1, y0, y1, ln2_g[layer].reshape(1, d), ln2_b[layer].reshape(1, d),
                       ple_w_gate[layer].astype(BF), p_cat, ple_w_proj[layer].astype(BF))

    w_in = lru_w_in[0].astype(BF)
    x0b = x0.astype(BF)
    gate = _mm_act(x0b, w_in[:, :d_rnn], _gelu)
    xr = _mm_act(x0b, w_in[:, d_rnn:], _identity)
    lru_args = (lru_conv_w[0], lru_conv_b[0].reshape(1, d_rnn), lru_w_a[0].astype(BF), lru_b_a[0].reshape(1, d_rnn),
                lru_w_x[0].astype(BF), lru_b_x[0].reshape(1, d_rnn), lru_lambda[0].reshape(1, d_rnn))
    hg_p, hl_p = _rglru_core(xr, gate, jnp.zeros((1, SUBLANES, d_rnn), F32), jnp.zeros((1, 1, d_rnn), F32),
                             *lru_args, chain=True, tt=_row_tile(n_p, 512), row0=0, n=n_p)
    prev_s = jnp.pad(state_conv[0], ((0, 0), (SUBLANES - (CONV_W - 1), 0), (0, 0)))
    hg_s, hl_s = _rglru_core(xr, gate, prev_s, state_h[0].reshape(bs, 1, d_rnn),
                             *lru_args, chain=False, tt=n_s, row0=n_p, n=n_s)
    hg = jnp.concatenate([hg_p, hg_s], axis=0)
    conv_prompt = xr[n_p - (CONV_W - 1):n_p].reshape(1, bp, CONV_W - 1, d_rnn)
    h_prompt = hl_p[n_p // SUBLANES - 1].reshape(1, bp, d_rnn)
    conv_sample = xr[n_p:].reshape(bs, ts, d_rnn)[:, ts - (CONV_W - 1):].reshape(1, bs, CONV_W - 1, d_rnn)
    h_sample = hl_s.reshape(1, bs, d_rnn)
    x1 = _mm_res_ln(hg, lru_w_out[0].astype(BF), x0, ln1_g[0].reshape(1, d), ln1_b[0].reshape(1, d))
    xa = moe_and_ple(x1, 0)

    w_nsa = nsa_w_in[0]
    n_main = hd + 3 * kvw
    pos = jnp.concatenate([jnp.arange(n_p, dtype=jnp.int32) % tp,
                           past_len + jnp.arange(n_s, dtype=jnp.int32) % ts])
    half = HEAD_DIM // 2
    inv = ROPE_THETA ** (-jnp.arange(half, dtype=F32) / half)
    ang = pos.astype(F32)[:, None] * inv[None, :]
    cos = jnp.concatenate([jnp.cos(ang), jnp.cos(ang)], axis=1)
    sin_signed = jnp.concatenate([-jnp.sin(ang), jnp.sin(ang)], axis=1)
    xab = xa.astype(BF)
    u5, ukv = _nsa_in(xab, w_nsa[:, :n_main].astype(BF), cos, sin_signed)
    w_gates = jnp.zeros((d, LANES), F32).at[:, :3 * N_HEADS].set(w_nsa[:, n_main:]).astype(BF)
    gates = _mm_act(xab, w_gates, jax.nn.sigmoid)

    kv_prompt = u5[:n_p, hd:hd + 2 * kvw].reshape(1, bp, tp, 4, N_KV, HEAD_DIM)
    kv_sample = u5[n_p:, hd:hd + 2 * kvw].reshape(1, bs, ts, 4, N_KV, HEAD_DIM)
    win_new_p = u5[:n_p, hd + 2 * kvw:]
    win_new_s = u5[n_p:, hd + 2 * kvw:].reshape(bs, ts, kvw)
    win_keep_p = min(WINDOW, tp)
    win_prompt = win_new_p[n_p - win_keep_p:].reshape(1, bp, win_keep_p, 2, N_KV, HEAD_DIM)
    win_keep_s = cache_win.shape[2]
    win_sample = jnp.concatenate([cache_win[0][:, ts:], win_new_s.reshape(bs, ts, 2, N_KV, HEAD_DIM)], axis=1)[None]

    kd = CMP_STRIDE * HEAD_DIM
    w1 = nsa_cmp_w1[0]
    w1flat = w1.reshape(2, 2 * kd, HEAD_DIM).astype(BF)
    w1cat = jnp.concatenate([w1flat[:, :kd], w1flat[:, kd:]], axis=2)
    pe8 = jnp.zeros((2, SUBLANES, 2 * kd), F32).at[:, 0].set(nsa_cmp_pe[0].reshape(2, 2 * kd))
    w2 = nsa_cmp_w2[0].astype(BF)

    kcvc_p = _compress(u5[:n_p, hd:hd + kvw].reshape(bp, tp, kvw), w1cat, pe8, w1flat, w2)
    tq_p = _row_tile(tp, 128)
    oc_p, sb_p = _cmp_attn(u5, kcvc_p, nseq=bp, tq_total=tp, tq=tq_p, row0=0, t_kv=tp, pos0=0)
    ksvs_p = ukv[:n_p, :kvw]
    kwvw_p = ukv[:n_p, kvw:]
    o_p = _sel_win_attn(u5, sb_p, gates, oc_p, ksvs_p, kwvw_p, tq=_row_tile(tp, 256), tk=_row_tile(tp, 1024))

    page_size = cache_kv.shape[2]
    cache_tok = cache_kv[0].reshape(-1, ROWS_PER_TOKEN, HEAD_DIM)
    pcmp = _compress_paged(page_table, cache_tok, w1cat, page_size=page_size)
    t_kv_s = past_len + ts
    oc_s, sb_s = _cmp_attn_paged(u5, pcmp, pe8, w1flat, w2, tq=ts, row0=n_p, t_kv=t_kv_s, pos0=past_len)
    nsp_s = sb_s.shape[1] // N_KV
    q5 = u5[n_p:, :hd].reshape(bs, ts, N_KV, HPG, HEAD_DIM)
    eye = jnp.eye(N_KV, dtype=F32)[None, None, :, None, :, None]
    q_diag = (q5[:, :, :, :, None, :] * eye).transpose(0, 2, 3, 1, 4, 5).reshape(bs, N_HEADS * ts, N_KV * HEAD_DIM)
    bias = jnp.broadcast_to(sb_s.reshape(bs, ts, N_KV, 1, nsp_s).transpose(0, 2, 3, 1, 4),
                            (bs, N_KV, HPG, ts, nsp_s)).reshape(bs, N_HEADS * ts, nsp_s)
    wq = jnp.concatenate([q_diag, bias], axis=2).astype(BF)
    o_s = _sel_win_paged(page_table, cache_tok, wq, u5, cache_win[0].reshape(-1, HEAD_DIM), gates, oc_s,
                         tq=ts, row0=n_p, page_size=page_size, win_keep=win_keep_s, past_len=past_len)

    o_all = jnp.concatenate([o_p, o_s], axis=0)
    x1 = _mm_res_ln(o_all, nsa_w_out[0].astype(BF), xa, ln1_g[1].reshape(1, d), ln1_b[1].reshape(1, d))
    xo = moe_and_ple(x1, 1)

    y_prompt = xo[:n_p].reshape(bp, tp, d)
    y_sample = xo[n_p:].reshape(bs, ts, d)
    return (y_prompt, y_sample, conv_prompt, h_prompt, kv_prompt, win_prompt,
            conv_sample, h_sample, kv_sample, win_sample)
```

```python
import functools

import jax
import jax.numpy as jnp
from jax import lax
from jax.experimental import pallas as pl
from jax.experimental.pallas import tpu as pltpu

F32 = jnp.float32
BF = jnp.bfloat16

HEAD_DIM = 128
N_HEADS = 16
N_KV = 4
HPG = N_HEADS // N_KV
LRU_BLOCK = 256
CONV_W = 4
LRU_C = 8.0
CMP_BLOCK = 32
CMP_STRIDE = 16
SEL_BLOCK = 64
N_SEL = 16
WINDOW = 512
ROPE_THETA = 10000.0
N_GROUPS = 4
EPG = 8
N_EXPERTS = N_GROUPS * EPG
DEPTH = 2
ALPHA = (2 * DEPTH) ** 0.25
LN_EPS = 1e-5
NEG = -1e30
FORCE_BONUS = 1e4
SCALE = HEAD_DIM ** -0.5
SEL_SHIFT = SEL_BLOCK.bit_length() - 1
EPG_SHIFT = EPG.bit_length() - 1

LANES = 128
SUBLANES = 8
VMEM_LIMIT = 56 * 1024 * 1024
MOE_TM = 1024
MOE_SUB = 256
MOE_TF = 256


def _cparams(sem):
    return pltpu.CompilerParams(dimension_semantics=sem, vmem_limit_bytes=VMEM_LIMIT)


def _row_tile(n, cap):
    best = None
    for t in range(SUBLANES, cap + 1, SUBLANES):
        if n % t == 0:
            best = t
    assert best is not None, (n, cap)
    return best


def _mm_act_kernel(x_ref, w_ref, o_ref, *, act):
    u = jnp.dot(x_ref[...].astype(BF), w_ref[...], preferred_element_type=F32)
    o_ref[...] = act(u).astype(o_ref.dtype)


def _mm_act(x, w, act, out_dtype=F32, tn=512):
    m, k = x.shape
    n = w.shape[1]
    tm = _row_tile(m, 768)
    tn = min(tn, n)
    assert n % tn == 0
    return pl.pallas_call(
        functools.partial(_mm_act_kernel, act=act),
        grid=(m // tm, n // tn),
        in_specs=[pl.BlockSpec((tm, k), lambda i, j: (i, 0)),
                  pl.BlockSpec((k, tn), lambda i, j: (0, j))],
        out_specs=pl.BlockSpec((tm, tn), lambda i, j: (i, j)),
        out_shape=jax.ShapeDtypeStruct((m, n), out_dtype),
        compiler_params=_cparams(("parallel", "arbitrary")),
        name="mm_act",
    )(x, w)


def _rglru_kernel(xr_ref, prev_ref, h0_ref, gate_ref, cw_ref, cb_ref, wa_ref, ba_ref, wx_ref, bx_ref,
                  lam_ref, hg_ref, hlast_ref, carry_prev, carry_h, h_scr, *, chain):
    tt, c = xr_ref.shape
    g = tt // SUBLANES
    x = xr_ref[...].reshape(g, SUBLANES, c)
    if chain:
        first = pl.program_id(1) == 0
        p0 = jnp.where(first, prev_ref[...], carry_prev[...])
        prev = jnp.concatenate([p0, x[:-1]], axis=0) if g > 1 else p0
        h_start = jnp.where(first, h0_ref[...], carry_h[...])
    else:
        prev = prev_ref[...]
    sub = lax.broadcasted_iota(jnp.int32, (g, SUBLANES, c), 1)
    cw = [cw_ref[k:k + 1, :].reshape(1, 1, c) for k in range(CONV_W)]
    xc = cb_ref[...].reshape(1, 1, c) + x * cw[CONV_W - 1]
    for s in range(1, CONV_W):
        sh = jnp.where(sub < s, pltpu.roll(prev, s, 1), pltpu.roll(x, s, 1))
        xc = xc + sh * cw[CONV_W - 1 - s]
    xc2 = xc.reshape(tt, c)
    xb = xc2.astype(BF)
    r = jax.nn.sigmoid(jnp.dot(xb, wa_ref[...], preferred_element_type=F32) + ba_ref[...])
    i = jax.nn.sigmoid(jnp.dot(xb, wx_ref[...], preferred_element_type=F32) + bx_ref[...])
    log_a = (-LRU_C * r) * jax.nn.softplus(-lam_ref[...])
    a = jnp.exp(log_a)
    bterm = jnp.sqrt(-jnp.tanh(log_a) * (a * a + 1.0)) * (i * xc2)
    a3 = a.reshape(g, SUBLANES, c)
    b3 = bterm.reshape(g, SUBLANES, c)
    for d in (1, 2, 4):
        a_sh = jnp.where(sub < d, 1.0, pltpu.roll(a3, d, 1))
        b_sh = jnp.where(sub < d, 0.0, pltpu.roll(b3, d, 1))
        b3 = a3 * b_sh + b3
        a3 = a3 * a_sh
    if chain:
        hs = []
        hprev = h_start.reshape(1, c)
        for j in range(g):
            hj = a3[j] * hprev + b3[j]
            hs.append(hj)
            hprev = hj[SUBLANES - 1:SUBLANES, :]
        h3 = jnp.stack(hs, axis=0)
        carry_prev[...] = x[g - 1:g]
        carry_h[...] = hprev.reshape(1, 1, c)
    else:
        h3 = a3 * h0_ref[...] + b3
    h2 = h3.reshape(tt, c)
    hg_ref[...] = (h2 * gate_ref[...]).astype(hg_ref.dtype)
    for k in range(c // LANES):
        h_scr[k] = h2[:, k * LANES:(k + 1) * LANES]
        hlast_ref[:, k * LANES:(k + 1) * LANES] = h_scr[k, pl.ds(SUBLANES - 1, g, stride=SUBLANES), :]


def _rglru_core(xr, gate, prev, h0, cw, cb, wa, ba, wx, bx, lam, *, chain, tt, row0, n):
    d = xr.shape[1]
    rb0 = row0 // tt
    assert row0 % tt == 0 and n % tt == 0
    c = LRU_BLOCK
    g = tt // SUBLANES
    pg = 1 if chain else g
    vec = lambda: pl.BlockSpec((1, c), lambda nb, t: (0, nb))
    return pl.pallas_call(
        functools.partial(_rglru_kernel, chain=chain),
        grid=(d // c, n // tt),
        in_specs=[pl.BlockSpec((tt, c), lambda nb, t: (rb0 + t, nb)),
                  pl.BlockSpec((pg, SUBLANES, c), lambda nb, t: (0 if chain else t, 0, nb)),
                  pl.BlockSpec((pg, 1, c), lambda nb, t: (0 if chain else t, 0, nb)),
                  pl.BlockSpec((tt, c), lambda nb, t: (rb0 + t, nb)),
                  pl.BlockSpec((CONV_W, c), lambda nb, t: (0, nb)),
                  vec(),
                  pl.BlockSpec((None, c, c), lambda nb, t: (nb, 0, 0)),
                  vec(),
                  pl.BlockSpec((None, c, c), lambda nb, t: (nb, 0, 0)),
                  vec(), vec()],
        out_specs=[pl.BlockSpec((tt, c), lambda nb, t: (t, nb)),
                   pl.BlockSpec((g, c), lambda nb, t: (t, nb))],
        out_shape=[jax.ShapeDtypeStruct((n, d), BF), jax.ShapeDtypeStruct((n // SUBLANES, d), F32)],
        scratch_shapes=[pltpu.VMEM((1, SUBLANES, c), F32), pltpu.VMEM((1, 1, c), F32),
                        pltpu.VMEM((c // LANES, tt, LANES), F32)],
        compiler_params=_cparams(("parallel", "arbitrary")),
        name="rglru_core",
    )(xr, prev, h0, gate, cw, cb, wa, ba, wx, bx, lam)


def _layer_norm(z, g, b):
    mu = jnp.mean(z, axis=-1, keepdims=True)
    zc = z - mu
    var = jnp.mean(zc * zc, axis=-1, keepdims=True)
    return zc * lax.rsqrt(var + LN_EPS) * g + b


def _mm_res_ln_kernel(a_ref, w_ref, res_ref, g_ref, b_ref, o_ref):
    y = jnp.dot(a_ref[...].astype(BF), w_ref[...], preferred_element_type=F32)
    o_ref[...] = _layer_norm(ALPHA * res_ref[...] + y, g_ref[...], b_ref[...])


def _mm_res_ln(a, w, res, g, b):
    m, k = a.shape
    d = w.shape[1]
    tm = _row_tile(m, 256)
    return pl.pallas_call(
        _mm_res_ln_kernel,
        grid=(m // tm,),
        in_specs=[pl.BlockSpec((tm, k), lambda i: (i, 0)),
                  pl.BlockSpec((k, d), lambda i: (0, 0)),
                  pl.BlockSpec((tm, d), lambda i: (i, 0)),
                  pl.BlockSpec((1, d), lambda i: (0, 0)),
                  pl.BlockSpec((1, d), lambda i: (0, 0))],
        out_specs=pl.BlockSpec((tm, d), lambda i: (i, 0)),
        out_shape=jax.ShapeDtypeStruct((m, d), F32),
        compiler_params=_cparams(("parallel",)),
        name="mm_res_ln",
    )(a, w, res, g, b)


def _route_kernel(x_ref, w_ref, b_ref, o_ref):
    logits = jnp.dot(x_ref[...], w_ref[...], preferred_element_type=F32,
                     precision=lax.Precision.HIGHEST) + b_ref[...]
    tm = logits.shape[0]
    lane = lax.broadcasted_iota(jnp.int32, (tm, LANES), 1)
    lanef = lane.astype(F32)
    big = float(LANES)
    ninf = -jnp.inf
    is_grp = (lane >= N_EXPERTS) & (lane < N_EXPERTS + N_GROUPS)
    lg = jnp.where(is_grp, logits, ninf)
    mg = jnp.max(lg, axis=-1, keepdims=True)
    p_top = 1.0 / jnp.sum(jnp.where(is_grp, jnp.exp(lg - mg), 0.0), axis=-1, keepdims=True)
    g_sel = jnp.min(jnp.where(lg == mg, lanef, big), axis=-1, keepdims=True) - float(N_EXPERTS)
    in_grp = (lane < N_EXPERTS) & ((lane >> EPG_SHIFT).astype(F32) == g_sel)
    le = jnp.where(in_grp, logits, ninf)
    v1 = jnp.max(le, axis=-1, keepdims=True)
    j1 = jnp.min(jnp.where(le == v1, lanef, big), axis=-1, keepdims=True)
    le2 = jnp.where(lanef == j1, ninf, le)
    v2 = jnp.max(le2, axis=-1, keepdims=True)
    j2 = jnp.min(jnp.where(le2 == v2, lanef, big), axis=-1, keepdims=True)
    e2 = jnp.exp(v2 - v1)
    den = 1.0 + e2
    w1 = (1.0 / den) * p_top
    w2 = (e2 / den) * p_top
    out = jnp.where(lane == 0, j1, jnp.where(lane == 1, j2, jnp.where(lane == 2, w1, jnp.where(lane == 3, w2, 0.0))))
    o_ref[...] = out


def _route(x, w_cat, b_cat):
    m, d = x.shape
    tm = _row_tile(m, 256)
    return pl.pallas_call(
        _route_kernel,
        grid=(m // tm,),
        in_specs=[pl.BlockSpec((tm, d), lambda i: (i, 0)),
                  pl.BlockSpec((d, LANES), lambda i: (0, 0)),
                  pl.BlockSpec((1, LANES), lambda i: (0, 0))],
        out_specs=pl.BlockSpec((tm, LANES), lambda i: (i, 0)),
        out_shape=jax.ShapeDtypeStruct((m, LANES), F32),
        compiler_params=_cparams(("parallel",)),
        name="moe_route",
    )(x, w_cat, b_cat)


def _moe_kernel(te_ref, ns_ref, st_ref, tok_ref, x_hbm, wt_ref, wg_ref, wu_ref, wd_ref, o_ref, xbuf, sem):
    i = pl.program_id(0)
    f = pl.program_id(1)
    nsub = ns_ref[i]

    def row_copy(tok, r):
        return pltpu.make_async_copy(x_hbm.at[pl.ds(tok, 1)], xbuf.at[pl.ds(r, 1)], sem)

    @pl.when(f == 0)
    def _():
        o_ref[...] = jnp.zeros_like(o_ref)
        base = st_ref[i] * MOE_SUB
        nrows = nsub * MOE_SUB

        def issue(r8, carry):
            for u in range(SUBLANES):
                r = r8 * SUBLANES + u
                row_copy(tok_ref[base + r], r).start()
            return carry

        def drain(r8, carry):
            for u in range(SUBLANES):
                row_copy(0, r8 * SUBLANES + u).wait()
            return carry

        lax.fori_loop(0, nrows // SUBLANES, issue, 0)
        lax.fori_loop(0, nrows // SUBLANES, drain, 0)

    @pl.when(nsub > 0)
    def _():
        xs0 = xbuf[0:MOE_SUB, :].astype(BF)
        wg = wg_ref[...].astype(BF)
        hg0 = jnp.dot(xs0, wg, preferred_element_type=F32)
        wu = wu_ref[...].astype(BF)
        hu0 = jnp.dot(xs0, wu, preferred_element_type=F32)
        wd = wd_ref[...].astype(BF)
        h0 = (jax.nn.silu(hg0) * hu0) * wt_ref[0:MOE_SUB, :]
        o_ref[0:MOE_SUB, :] += jnp.dot(h0.astype(BF), wd, preferred_element_type=F32)

        def body(s, carry):
            r0 = pl.multiple_of(s * MOE_SUB, MOE_SUB)
            xs = xbuf[pl.ds(r0, MOE_SUB), :].astype(BF)
            hg = jnp.dot(xs, wg, preferred_element_type=F32)
            hu = jnp.dot(xs, wu, preferred_element_type=F32)
            h = (jax.nn.silu(hg) * hu) * wt_ref[pl.ds(r0, MOE_SUB), :]
            o_ref[pl.ds(r0, MOE_SUB), :] += jnp.dot(h.astype(BF), wd, preferred_element_type=F32)
            return carry

        lax.fori_loop(1, nsub, body, 0)


def _moe_ffn(tile_expert, tile_nsub, tile_start, tok_pad, x, wt, w_gate, w_up, w_down, layer):
    _, d = x.shape
    nt = tile_expert.shape[0]
    ff = w_gate.shape[3]
    grid_spec = pltpu.PrefetchScalarGridSpec(
        num_scalar_prefetch=4,
        grid=(nt, ff // MOE_TF),
        in_specs=[pl.BlockSpec(memory_space=pl.ANY),
                  pl.BlockSpec((pl.Element(MOE_TM), pl.Element(1)), lambda i, f, te, ns, st, tk: (st[i] * MOE_SUB, 0)),
                  pl.BlockSpec((None, None, d, MOE_TF), lambda i, f, te, ns, st, tk: (layer, te[i], 0, f)),
                  pl.BlockSpec((None, None, d, MOE_TF), lambda i, f, te, ns, st, tk: (layer, te[i], 0, f)),
                  pl.BlockSpec((None, None, MOE_TF, d), lambda i, f, te, ns, st, tk: (layer, te[i], f, 0))],
        out_specs=pl.BlockSpec((MOE_TM, d), lambda i, f, te, ns, st, tk: (i, 0)),
        scratch_shapes=[pltpu.VMEM((MOE_TM, d), F32), pltpu.SemaphoreType.DMA(())],
    )
    return pl.pallas_call(
        _moe_kernel,
        grid_spec=grid_spec,
        out_shape=jax.ShapeDtypeStruct((nt * MOE_TM, d), F32),
        compiler_params=_cparams(("arbitrary", "arbitrary")),
        name="moe_ffn",
    )(tile_expert, tile_nsub, tile_start, tok_pad, x, wt, w_gate, w_up, w_down)


def _ln_ple_kernel(x1_ref, y0_ref, y1_ref, g_ref, b_ref, wg_ref, p_ref, wp_ref, o_ref):
    z = ALPHA * x1_ref[...] + (y0_ref[...] + y1_ref[...])
    x2 = _layer_norm(z, g_ref[...], b_ref[...])
    gate = jax.nn.sigmoid(jnp.dot(x2.astype(BF), wg_ref[...], preferred_element_type=F32))
    proj = jnp.dot(p_ref[...].astype(BF), wp_ref[...], preferred_element_type=F32)
    o_ref[...] = x2 + gate * proj


def _ln_ple(x1, y0, y1, g, b, w_gate, p, w_proj):
    m, d = x1.shape
    pd = p.shape[1]
    tm = _row_tile(m, 256)
    row = lambda w: pl.BlockSpec((tm, w), lambda i: (i, 0))
    return pl.pallas_call(
        _ln_ple_kernel,
        grid=(m // tm,),
        in_specs=[row(d), row(d), row(d),
                  pl.BlockSpec((1, d), lambda i: (0, 0)),
                  pl.BlockSpec((1, d), lambda i: (0, 0)),
                  pl.BlockSpec((d, d), lambda i: (0, 0)),
                  row(pd),
                  pl.BlockSpec((pd, d), lambda i: (0, 0))],
        out_specs=row(d),
        out_shape=jax.ShapeDtypeStruct((m, d), F32),
        compiler_params=_cparams(("parallel",)),
        name="ln_ple",
    )(x1, y0, y1, g, b, w_gate, p, w_proj)


def _moe_layer(x1, w_route, b_route, w_gate, w_up, w_down, layer):
    n, d = x1.shape
    routed = _route(x1, w_route, b_route)
    e_tok = routed[:, 0:2].astype(jnp.int32)
    w_tok = routed[:, 2:4]
    a = 2 * n
    sub_per_tile = MOE_TM // MOE_SUB
    nt = a // MOE_TM + N_EXPERTS
    n_sub_max = a // MOE_SUB + N_EXPERTS
    e_flat = e_tok.reshape(a)
    onehot = (e_flat[:, None] == jnp.arange(N_EXPERTS, dtype=jnp.int32)[None, :]).astype(jnp.int32)
    csum = jnp.cumsum(onehot, axis=0)
    counts = csum[a - 1]
    rank = jnp.sum(csum * onehot, axis=1) - 1
    sub_per = (counts + MOE_SUB - 1) // MOE_SUB
    sub_off = jnp.cumsum(sub_per) - sub_per
    tiles_per = (counts + MOE_TM - 1) // MOE_TM
    tile_end = jnp.cumsum(tiles_per)
    tile_off = tile_end - tiles_per
    src_row = sub_off[e_flat] * MOE_SUB + rank
    dst_row = (tile_off[e_flat] + rank // MOE_TM) * MOE_TM + rank % MOE_TM
    rows_in = (n_sub_max + sub_per_tile) * MOE_SUB
    asg_pad = jnp.full((rows_in,), -1, jnp.int32).at[src_row].set(jnp.arange(a, dtype=jnp.int32))
    asg = jnp.maximum(asg_pad, 0)
    tok_pad = jnp.where(asg_pad >= 0, asg // 2, 0)
    w_pad = jnp.where(asg_pad >= 0, w_tok.reshape(a)[asg], 0.0)
    tile_id = jnp.arange(nt, dtype=jnp.int32)
    n_used = tile_end[N_EXPERTS - 1]
    tile_expert = jnp.minimum(jnp.sum((tile_end[None, :] <= tile_id[:, None]).astype(jnp.int32), axis=1),
                              N_EXPERTS - 1)
    k_in_expert = tile_id - tile_off[tile_expert]
    used = tile_id < n_used
    tile_nsub = jnp.where(used, jnp.clip(sub_per[tile_expert] - k_in_expert * sub_per_tile, 0, sub_per_tile), 0)
    tile_start = jnp.where(used, sub_off[tile_expert] + k_in_expert * sub_per_tile, 0)
    last_used = jnp.sum(jnp.where(tile_id == n_used - 1, tile_expert, 0))
    tile_expert = jnp.where(used, tile_expert, last_used)
    yg = _moe_ffn(tile_expert.astype(jnp.int32), tile_nsub.astype(jnp.int32), tile_start.astype(jnp.int32),
                  tok_pad, x1, w_pad.reshape(-1, 1), w_gate, w_up, w_down, layer)
    dst2 = dst_row.reshape(n, 2)
    return (yg.at[dst2[:, 0]].get(mode="promise_in_bounds"), yg.at[dst2[:, 1]].get(mode="promise_in_bounds"))


def _nsa_in_kernel(x_ref, w_ref, cos_ref, sin_ref, o_ref, ob_ref, okv_ref, *, n_q_blocks):
    j = pl.program_id(1)
    tm = x_ref.shape[0]
    u = jnp.dot(x_ref[...].astype(BF), w_ref[...], preferred_element_type=F32)
    cos = cos_ref[...]
    sin = sin_ref[...]
    heads = u.shape[1] // HEAD_DIM
    for h in range(heads):
        seg = u[:, h * HEAD_DIM:(h + 1) * HEAD_DIM]
        rot = seg * cos + pltpu.roll(seg, HEAD_DIM // 2, 1) * sin
        if h < heads // 2:
            val = rot
        else:
            val = jnp.where(j < n_q_blocks, rot, seg)
        o_ref[:, h * HEAD_DIM:(h + 1) * HEAD_DIM] = val

    @pl.when(j > n_q_blocks)
    def _():
        ob_ref[...] = o_ref[...].astype(BF)

    for blk in range(2):
        @pl.when(j == n_q_blocks + blk)
        def _():
            for h in range(heads):
                okv_ref[pl.ds(blk * heads + h, tm, stride=ROWS_PER_TOKEN), :] = o_ref[:, h * HEAD_DIM:(h + 1) * HEAD_DIM]


def _nsa_in(xb, w, cos, sin_signed):
    m, k = xb.shape
    n = w.shape[1]
    tn = 2 * N_KV * HEAD_DIM
    tm = _row_tile(m, 768)
    nqb = (N_HEADS * HEAD_DIM) // tn
    return pl.pallas_call(
        functools.partial(_nsa_in_kernel, n_q_blocks=nqb),
        grid=(m // tm, n // tn),
        in_specs=[pl.BlockSpec((tm, k), lambda i, j: (i, 0)),
                  pl.BlockSpec((k, tn), lambda i, j: (0, j)),
                  pl.BlockSpec((tm, HEAD_DIM), lambda i, j: (i, 0)),
                  pl.BlockSpec((tm, HEAD_DIM), lambda i, j: (i, 0))],
        out_specs=[pl.BlockSpec((tm, tn), lambda i, j: (i, j)),
                   pl.BlockSpec((tm, tn), lambda i, j: (i, jnp.maximum(j - (nqb + 1), 0))),
                   pl.BlockSpec((tm * ROWS_PER_TOKEN, HEAD_DIM), lambda i, j: (i, 0))],
        out_shape=[jax.ShapeDtypeStruct((m, n), F32), jax.ShapeDtypeStruct((m, n - (nqb + 1) * tn), BF),
                   jax.ShapeDtypeStruct((m * ROWS_PER_TOKEN, HEAD_DIM), F32)],
        compiler_params=_cparams(("parallel", "arbitrary")),
        name="nsa_in",
    )(xb, w, cos, sin_signed)


def _compress_kernel(rows_ref, w1_ref, pe_ref, w1f_ref, w2_ref, o_ref, a_ref):
    t = rows_ref.shape[0]
    nch = t // CMP_STRIDE
    for s in range(CMP_STRIDE):
        a_ref[:, s * HEAD_DIM:(s + 1) * HEAD_DIM] = rows_ref[pl.ds(s, nch, stride=CMP_STRIDE), :].astype(BF)
    p = jnp.dot(a_ref[...], w1_ref[...], preferred_element_type=F32)
    pe_term = jnp.dot(pe_ref[...].astype(BF), w1f_ref[...], preferred_element_type=F32)[0:1, :]
    pre = p[:, :HEAD_DIM] + pltpu.roll(p[:, HEAD_DIM:], nch - 1, 0) + pe_term
    o_ref[...] = jnp.dot(jax.nn.gelu(pre).astype(BF), w2_ref[...], preferred_element_type=F32).astype(o_ref.dtype)


def _compress(rows, w1cat, pe8, w1flat, w2):
    nseq, t, _ = rows.shape
    nch = t // CMP_STRIDE
    kd = CMP_STRIDE * HEAD_DIM
    return pl.pallas_call(
        _compress_kernel,
        grid=(nseq, 2, N_KV),
        in_specs=[pl.BlockSpec((None, t, HEAD_DIM), lambda b, w, g: (b, 0, w * N_KV + g)),
                  pl.BlockSpec((None, kd, 2 * HEAD_DIM), lambda b, w, g: (w, 0, 0)),
                  pl.BlockSpec((None, SUBLANES, 2 * kd), lambda b, w, g: (w, 0, 0)),
                  pl.BlockSpec((None, 2 * kd, HEAD_DIM), lambda b, w, g: (w, 0, 0)),
                  pl.BlockSpec((None, HEAD_DIM, HEAD_DIM), lambda b, w, g: (w, 0, 0))],
        out_specs=pl.BlockSpec((None, None, None, nch, HEAD_DIM), lambda b, w, g: (b, w, g, 0, 0)),
        out_shape=jax.ShapeDtypeStruct((nseq, 2, N_KV, nch, HEAD_DIM), BF),
        scratch_shapes=[pltpu.VMEM((nch, kd), BF)],
        compiler_params=_cparams(("parallel", "arbitrary", "arbitrary")),
        name="compress",
    )(rows, w1cat, pe8, w1flat, w2)


def _stack_heads(qblk, tq):
    return jnp.concatenate([qblk[:, h * HEAD_DIM:(h + 1) * HEAD_DIM] for h in range(HPG)], axis=0)


def _cmp_attn_kernel(q_ref, kc_ref, vc_ref, oc_ref, sb_ref, *, tq, nsp, ns, nc, pos0):
    q0 = pos0 + pl.program_id(1) * tq
    _cmp_attn_body(q_ref, [kc_ref[g] for g in range(N_KV)], [vc_ref[g] for g in range(N_KV)], oc_ref, sb_ref, q0,
                   tq=tq, nsp=nsp, ns=ns, nc=nc)


def _cmp_attn_body(q_ref, kcs, vcs, oc_ref, sb_ref, q0, *, tq, nsp, ns, nc):
    ncp = kcs[0].shape[0]
    rows = HPG * tq
    gw = HPG * HEAD_DIM
    nt = (((1,), (1,)), ((), ()))
    qs = [_stack_heads(q_ref[:, g * gw:(g + 1) * gw], tq).astype(BF) for g in range(N_KV)]
    ss = [lax.dot_general(qs[g], kcs[g], nt, preferred_element_type=F32) * SCALE for g in range(N_KV)]
    t = q0 + (lax.broadcasted_iota(jnp.int32, (rows, ncp), 0) & (tq - 1))
    n = lax.broadcasted_iota(jnp.int32, (rows, ncp), 1)
    mask = (n * CMP_STRIDE + (CMP_BLOCK - 1) <= t) & (n < nc)
    ps = []
    for g in range(N_KV):
        sm = jnp.where(mask, ss[g], NEG)
        m = jnp.max(sm, axis=-1, keepdims=True)
        e = jnp.where(mask, jnp.exp(sm - m), 0.0)
        l = jnp.sum(e, axis=-1, keepdims=True)
        ps.append(e / jnp.where(l > 0.0, l, 1.0))
    ocs = [jnp.dot(ps[g].astype(BF), vcs[g], preferred_element_type=F32) for g in range(N_KV)]
    for g in range(N_KV):
        for h in range(HPG):
            c0 = (g * HPG + h) * HEAD_DIM
            oc_ref[:, c0:c0 + HEAD_DIM] = ocs[g][h * tq:(h + 1) * tq]
    nn = lax.broadcasted_iota(jnp.int32, (ncp, nsp), 0)
    bb = lax.broadcasted_iota(jnp.int32, (ncp, nsp), 1)
    dd = nn - bb * (SEL_BLOCK // CMP_STRIDE)
    ov = ((dd >= 1 - CMP_BLOCK // CMP_STRIDE) & (dd < SEL_BLOCK // CMP_STRIDE) & (nn < nc)).astype(BF)
    bcol = lax.broadcasted_iota(jnp.int32, (tq, nsp), 1)
    bcolf = bcol.astype(F32)
    tcol = q0 + lax.broadcasted_iota(jnp.int32, (tq, nsp), 0)
    causal = bcol * SEL_BLOCK <= tcol
    forced = (bcol == (tcol >> SEL_SHIFT)) | (bcol == 0)
    scores = []
    for g in range(N_KV):
        p = ps[g]
        imp = p[0:tq] + p[tq:2 * tq] + p[2 * tq:3 * tq] + p[3 * tq:4 * tq]
        hi = imp.astype(BF)
        r1 = imp - hi.astype(F32)
        mid = r1.astype(BF)
        lo = (r1 - mid.astype(F32)).astype(BF)
        imp_sel = (jnp.dot(hi, ov, preferred_element_type=F32) + jnp.dot(mid, ov, preferred_element_type=F32)
                   + jnp.dot(lo, ov, preferred_element_type=F32))
        score = jnp.where(causal, imp_sel + jnp.where(forced, FORCE_BONUS, 0.0), NEG)
        scores.append(jnp.where(bcol < ns, score, -3e38))
    sels = [jnp.zeros((tq, nsp), jnp.bool_) for _ in range(N_KV)]
    for _ in range(min(N_SEL, ns)):
        mxs = [jnp.max(scores[g], axis=-1, keepdims=True) for g in range(N_KV)]
        idxs = [jnp.min(jnp.where(scores[g] == mxs[g], bcolf, float(nsp)), axis=-1, keepdims=True)
                for g in range(N_KV)]
        for g in range(N_KV):
            pick = bcolf == idxs[g]
            sels[g] = sels[g] | pick
            scores[g] = jnp.where(pick, -jnp.inf, scores[g])
    for g in range(N_KV):
        sb_ref[:, g * nsp:(g + 1) * nsp] = jnp.where(sels[g], 0.0, NEG)


def _cmp_attn(u5, kcvc, *, nseq, tq_total, tq, row0, t_kv, pos0):
    ncp = kcvc.shape[3]
    nc = (t_kv - CMP_BLOCK) // CMP_STRIDE + 1
    ns = -(-t_kv // SEL_BLOCK)
    nsp = -(-ns // LANES) * LANES
    nqt = tq_total // tq
    rb0 = row0 // tq
    hd = N_HEADS * HEAD_DIM
    kern = functools.partial(_cmp_attn_kernel, tq=tq, nsp=nsp, ns=ns, nc=nc, pos0=pos0)
    return pl.pallas_call(
        kern,
        grid=(nseq, nqt),
        in_specs=[pl.BlockSpec((tq, hd), lambda b, i: (rb0 + b * nqt + i, 0)),
                  pl.BlockSpec((None, None, N_KV, ncp, HEAD_DIM), lambda b, i: (b, 0, 0, 0, 0)),
                  pl.BlockSpec((None, None, N_KV, ncp, HEAD_DIM), lambda b, i: (b, 1, 0, 0, 0))],
        out_specs=[pl.BlockSpec((tq, hd), lambda b, i: (b * nqt + i, 0)),
                   pl.BlockSpec((tq, N_KV * nsp), lambda b, i: (b * nqt + i, 0))],
        out_shape=[jax.ShapeDtypeStruct((nseq * tq_total, hd), F32),
                   jax.ShapeDtypeStruct((nseq * tq_total, N_KV * nsp), F32)],
        compiler_params=_cparams(("parallel", "arbitrary")),
        name="cmp_attn",
    )(u5, kcvc, kcvc)


EXP2_SCALE = SCALE * 1.4426950408889634


def _sel_win_kernel(q_ref, sb_ref, gt_ref, oc_ref, ks_ref, vs_ref, kw_ref, vw_ref, o_ref,
                    m_ref, acc_ref, s_ref, *, tq, nsp, tk):
    g = pl.program_id(1)
    q0 = pl.program_id(2) * tq
    rows = HPG * tq
    nt = (((1,), (1,)), ((), ()))
    q = _stack_heads(q_ref[...], tq).astype(BF)
    bias = jnp.concatenate([sb_ref[...]] * HPG, axis=0).astype(BF)
    qa = jnp.concatenate([q, bias], axis=1)
    ones = jnp.ones((tk, HEAD_DIM), BF)

    m_ref[...] = jnp.full(m_ref.shape, -jnp.inf, F32)
    acc_ref[...] = jnp.zeros(acc_ref.shape, F32)

    def scores(j):
        k0 = pl.multiple_of(j * tk, tk)
        blk = (k0 + lax.broadcasted_iota(jnp.int32, (tk, nsp), 0)) >> SEL_SHIFT
        onehot = (blk == lax.broadcasted_iota(jnp.int32, (tk, nsp), 1)).astype(BF)
        ka = jnp.concatenate([ks_ref[pl.ds(k0, tk), :], onehot], axis=1)
        return [lax.dot_general(qa[h * tq:(h + 1) * tq], ka, nt, preferred_element_type=F32) for h in range(HPG)]

    def consume(ss, j, masked):
        k0 = pl.multiple_of(j * tk, tk)
        va = jnp.concatenate([vs_ref[pl.ds(k0, tk), :], ones], axis=1)
        if masked:
            ok = (k0 + lax.broadcasted_iota(jnp.int32, (tq, tk), 1)
                  <= q0 + lax.broadcasted_iota(jnp.int32, (tq, tk), 0))
            ss = [jnp.where(ok, sh, NEG) for sh in ss]
        ps, alphas = [], []
        for h in range(HPG):
            rs = slice(h * tq, (h + 1) * tq)
            m_old = m_ref[rs]
            m_new = jnp.maximum(m_old, jnp.max(ss[h], axis=-1, keepdims=True))
            alphas.append(jnp.exp2((m_old - m_new) * EXP2_SCALE))
            ps.append(jnp.exp2((ss[h] - m_new) * EXP2_SCALE).astype(BF))
            m_ref[rs] = m_new
        for h in range(HPG):
            rs = slice(h * tq, (h + 1) * tq)
            acc_ref[rs] = alphas[h] * acc_ref[rs] + jnp.dot(ps[h], va, preferred_element_type=F32)

    j_last = (q0 + tq - 1) // tk
    for h, sh in enumerate(scores(0)):
        s_ref[h] = sh

    def sel_body(j, carry):
        cur = [s_ref[h] for h in range(HPG)]
        nxt = scores(j + 1)
        consume(cur, j, False)
        for h in range(HPG):
            s_ref[h] = nxt[h]
        return carry

    lax.fori_loop(0, j_last, sel_body, 0)
    nkw = WINDOW + tq
    w0 = pl.multiple_of(jnp.maximum(q0 - WINDOW, 0), tq)
    sw = lax.dot_general(q, kw_ref[pl.ds(w0, nkw), :], nt, preferred_element_type=F32)
    consume([s_ref[h] for h in range(HPG)], j_last, True)
    acc = acc_ref[...]
    o_s = acc[:, :HEAD_DIM] / acc[:, HEAD_DIM:HEAD_DIM + 1]
    vwa = jnp.concatenate([vw_ref[pl.ds(w0, nkw), :], jnp.ones((nkw, HEAD_DIM), BF)], axis=1)
    t = q0 + (lax.broadcasted_iota(jnp.int32, (rows, nkw), 0) & (tq - 1))
    kpos = w0 + lax.broadcasted_iota(jnp.int32, (rows, nkw), 1)
    sw = jnp.where((kpos <= t) & (kpos > t - WINDOW), sw, NEG)
    pw = jnp.exp2((sw - jnp.max(sw, axis=-1, keepdims=True)) * EXP2_SCALE)
    aw = jnp.dot(pw.astype(BF), vwa, preferred_element_type=F32)
    o_w = aw[:, :HEAD_DIM] / aw[:, HEAD_DIM:HEAD_DIM + 1]

    gts = gt_ref[...]
    oc = oc_ref[...]
    lane = lax.broadcasted_iota(jnp.int32, gts.shape, 1)
    for h in range(HPG):
        base = (g * HPG + h) * 3
        gc = jnp.sum(jnp.where(lane == base, gts, 0.0), axis=-1, keepdims=True)
        gs = jnp.sum(jnp.where(lane == base + 1, gts, 0.0), axis=-1, keepdims=True)
        gw = jnp.sum(jnp.where(lane == base + 2, gts, 0.0), axis=-1, keepdims=True)
        sl = slice(h * HEAD_DIM, (h + 1) * HEAD_DIM)
        rs = slice(h * tq, (h + 1) * tq)
        o_ref[:, sl] = gc * oc[:, sl] + gs * o_s[rs] + gw * o_w[rs]


def _sel_win_attn(u5, sbias, gates, oc, ksvs, kwvw, *, tq, tk):
    t = ksvs.shape[0]
    nsp = sbias.shape[1] // N_KV
    nqt = t // tq
    gw = HPG * HEAD_DIM
    assert t % tk == 0 and t % tq == 0 and t >= WINDOW + tq and WINDOW % tq == 0
    kern = functools.partial(_sel_win_kernel, tq=tq, nsp=nsp, tk=tk)
    return pl.pallas_call(
        kern,
        grid=(1, N_KV, nqt),
        in_specs=[pl.BlockSpec((tq, gw), lambda b, g, i: (i, g)),
                  pl.BlockSpec((tq, nsp), lambda b, g, i: (i, g)),
                  pl.BlockSpec((tq, LANES), lambda b, g, i: (i, 0)),
                  pl.BlockSpec((tq, gw), lambda b, g, i: (i, g)),
                  pl.BlockSpec((t, HEAD_DIM), lambda b, g, i: (0, g)),
                  pl.BlockSpec((t, HEAD_DIM), lambda b, g, i: (0, N_KV + g)),
                  pl.BlockSpec((t, HEAD_DIM), lambda b, g, i: (0, g)),
                  pl.BlockSpec((t, HEAD_DIM), lambda b, g, i: (0, N_KV + g))],
        out_specs=pl.BlockSpec((tq, gw), lambda b, g, i: (i, g)),
        out_shape=jax.ShapeDtypeStruct((t, N_KV * gw), F32),
        scratch_shapes=[pltpu.VMEM((HPG * tq, 1), F32), pltpu.VMEM((HPG * tq, 2 * HEAD_DIM), F32),
                        pltpu.VMEM((HPG, tq, tk), F32)],
        compiler_params=_cparams(("parallel", "arbitrary", "arbitrary")),
        name="sel_win_prompt",
    )(u5, sbias, gates, oc, ksvs, ksvs, kwvw, kwvw)


ROWS_PER_TOKEN = 4 * N_KV
CMP_PAGES_PER_STEP = 16


def _compress_paged_kernel(pt_ref, *refs, pg, page_size):
    page_refs = refs[:pg]
    w1_ref, o_ref, a_ref = refs[pg:]
    cpp = page_size // CMP_STRIDE
    for k in range(pg):
        xc = jnp.swapaxes(page_refs[k][...], 0, 1)
        for wg in range(2 * N_KV):
            w, g = divmod(wg, N_KV)
            r0 = (g * pg + k) * cpp
            ys = jnp.swapaxes(xc[wg].reshape(cpp, CMP_STRIDE, HEAD_DIM), 0, 1)
            for s in range(CMP_STRIDE):
                a_ref[w, r0:r0 + cpp, s * HEAD_DIM:(s + 1) * HEAD_DIM] = ys[s]
    for w in range(2):
        p = jnp.dot(a_ref[w].astype(BF), w1_ref[w], preferred_element_type=F32)
        for g in range(N_KV):
            o_ref[w, g] = p[g * pg * cpp:(g + 1) * pg * cpp]


def _compress_paged(page_table, cache_tok, w1cat, *, page_size):
    bs, n_pages = page_table.shape
    pg = CMP_PAGES_PER_STEP
    assert n_pages % pg == 0
    cpp = page_size // CMP_STRIDE
    kd = CMP_STRIDE * HEAD_DIM
    page_spec = lambda k: pl.BlockSpec((page_size, 2 * N_KV, HEAD_DIM), lambda b, j, pt: (pt[b, j * pg + k], 0, 0))
    grid_spec = pltpu.PrefetchScalarGridSpec(
        num_scalar_prefetch=1,
        grid=(bs, n_pages // pg),
        in_specs=[page_spec(k) for k in range(pg)] + [pl.BlockSpec((2, kd, 2 * HEAD_DIM), lambda b, j, pt: (0, 0, 0))],
        out_specs=pl.BlockSpec((None, 2, N_KV, pg * cpp, 2 * HEAD_DIM), lambda b, j, pt: (b, 0, 0, j, 0)),
        scratch_shapes=[pltpu.VMEM((2, N_KV * pg * cpp, kd), F32)],
    )
    return pl.pallas_call(
        functools.partial(_compress_paged_kernel, pg=pg, page_size=page_size),
        grid_spec=grid_spec,
        out_shape=jax.ShapeDtypeStruct((bs, 2, N_KV, n_pages * cpp, 2 * HEAD_DIM), F32),
        compiler_params=_cparams(("parallel", "arbitrary")),
        name="compress_paged",
    )(page_table, *([cache_tok] * pg), w1cat)


def _compress_finish(p, pe_term, w2):
    nch = p.shape[0]
    pre = p[:, :HEAD_DIM] + pltpu.roll(p[:, HEAD_DIM:], nch - 1, 0) + pe_term
    return jnp.dot(jax.nn.gelu(pre).astype(BF), w2, preferred_element_type=F32).astype(BF)


def _cmp_attn_paged_kernel(q_ref, pk_ref, pv_ref, pe_ref, w1f_ref, w2_ref, oc_ref, sb_ref, *, tq, nsp, ns, nc, pos0):
    pe_k = jnp.dot(pe_ref[0].astype(BF), w1f_ref[0], preferred_element_type=F32)[0:1, :]
    pe_v = jnp.dot(pe_ref[1].astype(BF), w1f_ref[1], preferred_element_type=F32)[0:1, :]
    kcs = [_compress_finish(pk_ref[g], pe_k, w2_ref[0]) for g in range(N_KV)]
    vcs = [_compress_finish(pv_ref[g], pe_v, w2_ref[1]) for g in range(N_KV)]
    _cmp_attn_body(q_ref, kcs, vcs, oc_ref, sb_ref, pos0, tq=tq, nsp=nsp, ns=ns, nc=nc)


def _cmp_attn_paged(u5, pcmp, pe8, w1flat, w2, *, tq, row0, t_kv, pos0):
    bs, _, _, nch, _ = pcmp.shape
    nc = (t_kv - CMP_BLOCK) // CMP_STRIDE + 1
    ns = -(-t_kv // SEL_BLOCK)
    nsp = -(-ns // LANES) * LANES
    rb0 = row0 // tq
    hd = N_HEADS * HEAD_DIM
    kd = CMP_STRIDE * HEAD_DIM
    kern = functools.partial(_cmp_attn_paged_kernel, tq=tq, nsp=nsp, ns=ns, nc=nc, pos0=pos0)
    return pl.pallas_call(
        kern,
        grid=(bs,),
        in_specs=[pl.BlockSpec((tq, hd), lambda b: (rb0 + b, 0)),
                  pl.BlockSpec((None, None, N_KV, nch, 2 * HEAD_DIM), lambda b: (b, 0, 0, 0, 0)),
                  pl.BlockSpec((None, None, N_KV, nch, 2 * HEAD_DIM), lambda b: (b, 1, 0, 0, 0)),
                  pl.BlockSpec((2, SUBLANES, 2 * kd), lambda b: (0, 0, 0)),
                  pl.BlockSpec((2, 2 * kd, HEAD_DIM), lambda b: (0, 0, 0)),
                  pl.BlockSpec((2, HEAD_DIM, HEAD_DIM), lambda b: (0, 0, 0))],
        out_specs=[pl.BlockSpec((tq, hd), lambda b: (b, 0)),
                   pl.BlockSpec((tq, N_KV * nsp), lambda b: (b, 0))],
        out_shape=[jax.ShapeDtypeStruct((bs * tq, hd), F32),
                   jax.ShapeDtypeStruct((bs * tq, N_KV * nsp), F32)],
        compiler_params=_cparams(("parallel",)),
        name="cmp_attn_paged",
    )(u5, pcmp, pcmp, pe8, w1flat, w2)


SEL_PAGES_PER_STEP = 16


def _sel_win_paged_kernel(pt_ref, *refs, pps, n_pages, page_size, nsp, tq, win_keep, past_len):
    page_refs = refs[:pps]
    wq_ref, newkv_ref, wcache_ref, newwin_ref, gt_ref, oc_ref, o_ref, m_ref, l_ref, acc_ref = refs[pps:]
    j = pl.program_id(1)
    n_steps = n_pages // pps
    rows = N_HEADS * tq
    kvd = N_KV * HEAD_DIM
    wq = wq_ref[...]
    nt = (((1,), (1,)), ((), ()))

    @pl.when(j == 0)
    def _():
        m_ref[...] = jnp.full(m_ref.shape, -jnp.inf, F32)
        l_ref[...] = jnp.zeros(l_ref.shape, F32)
        acc_ref[...] = jnp.zeros(acc_ref.shape, F32)

    def tile(k_all, v_all, first_block, mask):
        nk = k_all.shape[0]
        blk = first_block + (lax.broadcasted_iota(jnp.int32, (nk, nsp), 0) >> SEL_SHIFT)
        onehot = (blk == lax.broadcasted_iota(jnp.int32, (nk, nsp), 1)).astype(BF)
        s = lax.dot_general(wq, jnp.concatenate([k_all, onehot], axis=1), nt, preferred_element_type=F32)
        if mask is not None:
            s = jnp.where(mask, s, NEG)
        m_old = m_ref[...]
        m_new = jnp.maximum(m_old, jnp.max(s, axis=-1, keepdims=True))
        alpha = jnp.exp2((m_old - m_new) * EXP2_SCALE)
        p = jnp.exp2((s - m_new) * EXP2_SCALE)
        l_ref[...] = alpha * l_ref[...] + jnp.sum(p, axis=-1, keepdims=True)
        acc_ref[...] = alpha * acc_ref[...] + jnp.dot(p.astype(BF), v_all, preferred_element_type=F32)
        m_ref[...] = m_new

    def page_rows(which):
        tr = [jnp.swapaxes(ref[...], 0, 1) for ref in page_refs]
        return jnp.concatenate(
            [jnp.concatenate([t[which * N_KV + g] for g in range(N_KV)], axis=1) for t in tr],
            axis=0).astype(BF)

    @pl.when(j < n_steps)
    def _():
        tile(page_rows(0), page_rows(1), j * (pps * page_size // SEL_BLOCK), None)

    @pl.when(j == n_steps)
    def _():
        qi = lax.broadcasted_iota(jnp.int32, (rows, page_size), 0) & (tq - 1)
        key = lax.broadcasted_iota(jnp.int32, (rows, page_size), 1)
        nk = newkv_ref[...]
        zpad = jnp.zeros((page_size - tq, kvd), F32)
        k_all = jnp.concatenate([nk[:, :kvd], zpad], axis=0).astype(BF)
        v_all = jnp.concatenate([nk[:, kvd:], zpad], axis=0).astype(BF)
        tile(k_all, v_all, n_pages * (page_size // SEL_BLOCK), key <= qi)
        o_s = acc_ref[...] / l_ref[...]

        nw = newwin_ref[...]
        kc = jnp.concatenate([wcache_ref[pl.ds(g, win_keep, stride=2 * N_KV), :] for g in range(N_KV)], axis=1)
        vc = jnp.concatenate([wcache_ref[pl.ds(N_KV + g, win_keep, stride=2 * N_KV), :] for g in range(N_KV)], axis=1)
        kw = jnp.concatenate([kc, nw[:, :kvd], zpad], axis=0).astype(BF)
        vw = jnp.concatenate([vc, nw[:, kvd:], zpad], axis=0).astype(BF)
        nkw = win_keep + page_size
        sw = lax.dot_general(wq[:, :kvd], kw, nt, preferred_element_type=F32)
        t = past_len + (lax.broadcasted_iota(jnp.int32, (rows, nkw), 0) & (tq - 1))
        kpos = (past_len - win_keep) + lax.broadcasted_iota(jnp.int32, (rows, nkw), 1)
        sw = jnp.where((kpos <= t) & (kpos > t - WINDOW), sw, NEG)
        pw = jnp.exp2((sw - jnp.max(sw, axis=-1, keepdims=True)) * EXP2_SCALE)
        o_w = jnp.dot(pw.astype(BF), vw, preferred_element_type=F32) / jnp.sum(pw, axis=-1, keepdims=True)

        gts = gt_ref[...]
        oc = oc_ref[...]
        lane = lax.broadcasted_iota(jnp.int32, gts.shape, 1)
        for hh in range(N_HEADS):
            g = hh // HPG
            gc = jnp.sum(jnp.where(lane == hh * 3, gts, 0.0), axis=-1, keepdims=True)
            gs = jnp.sum(jnp.where(lane == hh * 3 + 1, gts, 0.0), axis=-1, keepdims=True)
            gw = jnp.sum(jnp.where(lane == hh * 3 + 2, gts, 0.0), axis=-1, keepdims=True)
            rs = slice(hh * tq, (hh + 1) * tq)
            cs = slice(g * HEAD_DIM, (g + 1) * HEAD_DIM)
            os_ = slice(hh * HEAD_DIM, (hh + 1) * HEAD_DIM)
            o_ref[:, os_] = gc * oc[:, os_] + gs * o_s[rs, cs] + gw * o_w[rs, cs]


def _sel_win_paged(page_table, cache_tok, wq, u5, wcache_flat, gates, oc, *, tq, row0, page_size, win_keep, past_len):
    bs, n_pages = page_table.shape
    pps = SEL_PAGES_PER_STEP
    assert n_pages % pps == 0
    n_steps = n_pages // pps
    rows = N_HEADS * tq
    kvd = N_KV * HEAD_DIM
    nsp = wq.shape[2] - kvd
    rb0 = row0 // tq
    sel_col = (N_HEADS * HEAD_DIM + 2 * kvd) // (2 * kvd)
    kern = functools.partial(_sel_win_paged_kernel, pps=pps, n_pages=n_pages, page_size=page_size, nsp=nsp, tq=tq,
                             win_keep=win_keep, past_len=past_len)
    page_spec = lambda k: pl.BlockSpec(
        (page_size, 2 * N_KV, HEAD_DIM), lambda b, j, pt: (pt[b, jnp.minimum(j, n_steps - 1) * pps + k], 1, 0))
    grid_spec = pltpu.PrefetchScalarGridSpec(
        num_scalar_prefetch=1,
        grid=(bs, n_steps + 1),
        in_specs=[page_spec(k) for k in range(pps)] + [
            pl.BlockSpec((None, rows, kvd + nsp), lambda b, j, pt: (b, 0, 0)),
            pl.BlockSpec((tq, 2 * kvd), lambda b, j, pt: (rb0 + b, sel_col)),
            pl.BlockSpec((win_keep * 2 * N_KV, HEAD_DIM), lambda b, j, pt: (b, 0)),
            pl.BlockSpec((tq, 2 * kvd), lambda b, j, pt: (rb0 + b, sel_col + 1)),
            pl.BlockSpec((tq, LANES), lambda b, j, pt: (rb0 + b, 0)),
            pl.BlockSpec((tq, N_HEADS * HEAD_DIM), lambda b, j, pt: (b, 0))],
        out_specs=pl.BlockSpec((tq, N_HEADS * HEAD_DIM), lambda b, j, pt: (b, 0)),
        scratch_shapes=[pltpu.VMEM((rows, 1), F32), pltpu.VMEM((rows, 1), F32), pltpu.VMEM((rows, kvd), F32)],
    )
    return pl.pallas_call(
        kern,
        grid_spec=grid_spec,
        out_shape=jax.ShapeDtypeStruct((bs * tq, N_HEADS * HEAD_DIM), F32),
        compiler_params=_cparams(("parallel", "arbitrary")),
        name="sel_win_paged",
    )(page_table, *([cache_tok] * pps), wq, u5, wcache_flat, u5, gates, oc)


def _gelu(u):
    return jax.nn.gelu(u)


def _identity(u):
    return u


def kernel(x_prompt, x_sample, state_conv, state_h, cache_kv, cache_win, page_table, p_prompt, p_sample,
           lru_w_in, lru_conv_w, lru_conv_b, lru_w_a, lru_b_a, lru_w_x, lru_b_x, lru_lambda, lru_w_out,
           nsa_w_in, nsa_cmp_w1, nsa_cmp_pe, nsa_cmp_w2, nsa_w_out,
           ln1_g, ln1_b, ln2_g, ln2_b,
           moe_w_group, moe_b_group, moe_w_expert, moe_b_expert, moe_w_gate, moe_w_up, moe_w_down,
           ple_w_gate, ple_w_proj):
    bp, tp, d = x_prompt.shape
    bs, ts, _ = x_sample.shape
    assert bp == 1 and ts == SUBLANES
    n_p = bp * tp
    n_s = bs * ts
    n = n_p + n_s
    past_len = page_table.shape[1] * cache_kv.shape[2]
    d_rnn = lru_w_out.shape[1]
    hd = N_HEADS * HEAD_DIM
    kvw = 2 * N_KV * HEAD_DIM

    x0 = jnp.concatenate([x_prompt.reshape(n_p, d), x_sample.reshape(n_s, d)], axis=0)

    def moe_and_ple(x1, layer):
        w_route = jnp.zeros((d, LANES), F32).at[:, :N_EXPERTS].set(moe_w_expert[layer])
        w_route = w_route.at[:, N_EXPERTS:N_EXPERTS + N_GROUPS].set(moe_w_group[layer])
        b_route = jnp.zeros((1, LANES), F32).at[0, :N_EXPERTS].set(moe_b_expert[layer])
        b_route = b_route.at[0, N_EXPERTS:N_EXPERTS + N_GROUPS].set(moe_b_group[layer])
        y0, y1 = _moe_layer(x1, w_route, b_route, moe_w_gate, moe_w_up, moe_w_down, layer)
        p_cat = jnp.concatenate([p_prompt[layer].reshape(n_p, -1), p_sample[layer].reshape(n_s, -1)], axis=0)
        return _ln_ple(x1, y0, y1, ln2_g[layer].reshape(1, d), ln2_b[layer].reshape(1, d),
                       ple_w_gate[layer].astype(BF), p_cat, ple_w_proj[layer].astype(BF))

    w_in = lru_w_in[0].astype(BF)
    x0b = x0.astype(BF)
    gate = _mm_act(x0b, w_in[:, :d_rnn], _gelu)
    xr = _mm_act(x0b, w_in[:, d_rnn:], _identity)
    lru_args = (lru_conv_w[0], lru_conv_b[0].reshape(1, d_rnn), lru_w_a[0].astype(BF), lru_b_a[0].reshape(1, d_rnn),
                lru_w_x[0].astype(BF), lru_b_x[0].reshape(1, d_rnn), lru_lambda[0].reshape(1, d_rnn))
    hg_p, hl_p = _rglru_core(xr, gate, jnp.zeros((1, SUBLANES, d_rnn), F32), jnp.zeros((1, 1, d_rnn), F32),
                             *lru_args, chain=True, tt=_row_tile(n_p, 512), row0=0, n=n_p)
    prev_s = jnp.pad(state_conv[0], ((0, 0), (SUBLANES - (CONV_W - 1), 0), (0, 0)))
    hg_s, hl_s = _rglru_core(xr, gate, prev_s, state_h[0].reshape(bs, 1, d_rnn),
                             *lru_args, chain=False, tt=n_s, row0=n_p, n=n_s)
    hg = jnp.concatenate([hg_p, hg_s], axis=0)
    conv_prompt = xr[n_p - (CONV_W - 1):n_p].reshape(1, bp, CONV_W - 1, d_rnn)
    h_prompt = hl_p[n_p // SUBLANES - 1].reshape(1, bp, d_rnn)
    conv_sample = xr[n_p:].reshape(bs, ts, d_rnn)[:, ts - (CONV_W - 1):].reshape(1, bs, CONV_W - 1, d_rnn)
    h_sample = hl_s.reshape(1, bs, d_rnn)
    x1 = _mm_res_ln(hg, lru_w_out[0].astype(BF), x0, ln1_g[0].reshape(1, d), ln1_b[0].reshape(1, d))
    xa = moe_and_ple(x1, 0)

    w_nsa = nsa_w_in[0]
    n_main = hd + 3 * kvw
    pos = jnp.concatenate([jnp.arange(n_p, dtype=jnp.int32) % tp,
                           past_len + jnp.arange(n_s, dtype=jnp.int32) % ts])
    half = HEAD_DIM // 2
    inv = ROPE_THETA ** (-jnp.arange(half, dtype=F32) / half)
    ang = pos.astype(F32)[:, None] * inv[None, :]
    cos = jnp.concatenate([jnp.cos(ang), jnp.cos(ang)], axis=1)
    sin_signed = jnp.concatenate([-jnp.sin(ang), jnp.sin(ang)], axis=1)
    xab = xa.astype(BF)
    u5, ukv, okv = _nsa_in(xab, w_nsa[:, :n_main].astype(BF), cos, sin_signed)
    w_gates = jnp.zeros((d, LANES), F32).at[:, :3 * N_HEADS].set(w_nsa[:, n_main:]).astype(BF)
    gates = _mm_act(xab, w_gates, jax.nn.sigmoid)

    kv_prompt = okv[:n_p * ROWS_PER_TOKEN].reshape(1, bp, tp, 4, N_KV, HEAD_DIM)
    kv_sample = okv[n_p * ROWS_PER_TOKEN:].reshape(1, bs, ts, 4, N_KV, HEAD_DIM)
    win_new_p = u5[:n_p, hd + 2 * kvw:]
    win_new_s = u5[n_p:, hd + 2 * kvw:].reshape(bs, ts, kvw)
    win_keep_p = min(WINDOW, tp)
    win_prompt = win_new_p[n_p - win_keep_p:].reshape(1, bp, win_keep_p, 2, N_KV, HEAD_DIM)
    win_keep_s = cache_win.shape[2]
    win_sample = jnp.concatenate([cache_win[0][:, ts:], win_new_s.reshape(bs, ts, 2, N_KV, HEAD_DIM)], axis=1)[None]

    kd = CMP_STRIDE * HEAD_DIM
    w1 = nsa_cmp_w1[0]
    w1flat = w1.reshape(2, 2 * kd, HEAD_DIM).astype(BF)
    w1cat = jnp.concatenate([w1flat[:, :kd], w1flat[:, kd:]], axis=2)
    pe8 = jnp.zeros((2, SUBLANES, 2 * kd), F32).at[:, 0].set(nsa_cmp_pe[0].reshape(2, 2 * kd))
    w2 = nsa_cmp_w2[0].astype(BF)

    kcvc_p = _compress(u5[:n_p, hd:hd + kvw].reshape(bp, tp, kvw), w1cat, pe8, w1flat, w2)
    tq_p = _row_tile(tp, 128)
    oc_p, sb_p = _cmp_attn(u5, kcvc_p, nseq=bp, tq_total=tp, tq=tq_p, row0=0, t_kv=tp, pos0=0)
    ksvs_p = ukv[:n_p, :kvw]
    kwvw_p = ukv[:n_p, kvw:]
    o_p = _sel_win_attn(u5, sb_p, gates, oc_p, ksvs_p, kwvw_p, tq=_row_tile(tp, 256), tk=_row_tile(tp, 1024))

    page_size = cache_kv.shape[2]
    cache_tok = cache_kv[0].reshape(-1, ROWS_PER_TOKEN, HEAD_DIM)
    pcmp = _compress_paged(page_table, cache_tok, w1cat, page_size=page_size)
    t_kv_s = past_len + ts
    oc_s, sb_s = _cmp_attn_paged(u5, pcmp, pe8, w1flat, w2, tq=ts, row0=n_p, t_kv=t_kv_s, pos0=past_len)
    nsp_s = sb_s.shape[1] // N_KV
    q5 = u5[n_p:, :hd].reshape(bs, ts, N_KV, HPG, HEAD_DIM)
    eye = jnp.eye(N_KV, dtype=F32)[None, None, :, None, :, None]
    q_diag = (q5[:, :, :, :, None, :] * eye).transpose(0, 2, 3, 1, 4, 5).reshape(bs, N_HEADS * ts, N_KV * HEAD_DIM)
    bias = jnp.broadcast_to(sb_s.reshape(bs, ts, N_KV, 1, nsp_s).transpose(0, 2, 3, 1, 4),
                            (bs, N_KV, HPG, ts, nsp_s)).reshape(bs, N_HEADS * ts, nsp_s)
    wq = jnp.concatenate([q_diag, bias], axis=2).astype(BF)
    o_s = _sel_win_paged(page_table, cache_tok, wq, u5, cache_win[0].reshape(-1, HEAD_DIM), gates, oc_s,
                         tq=ts, row0=n_p, page_size=page_size, win_keep=win_keep_s, past_len=past_len)

    o_all = jnp.concatenate([o_p, o_s], axis=0)
    x1 = _mm_res_ln(o_all, nsa_w_out[0].astype(BF), xa, ln1_g[1].reshape(1, d), ln1_b[1].reshape(1, d))
    xo = moe_and_ple(x1, 1)

    y_prompt = xo[:n_p].reshape(bp, tp, d)
    y_sample = xo[n_p:].reshape(bs, ts, d)
    return (y_prompt, y_sample, conv_prompt, h_prompt, kv_prompt, win_prompt,
            conv_sample, h_sample, kv_sample, win_sample)
```
